```python
import jax, jax.numpy as jnp
from jax import lax
import numpy as np

D_MODEL = 1024
BATCH = 2
SEQ = 8192
DEPTH = 4
DEC_BATCH = 16
DEC_SEQ = 16
PAST_LEN = 1024

CHUNK = 64
EPS = 1e-6
NEG_INF = -1e30
GDN_HEADS = 4
GDN_DK = 128
GDN_DV = 128
GDN_CONV = 4
GDN_QK = GDN_HEADS * GDN_DK
GDN_V = GDN_HEADS * GDN_DV
GDN_CONV_CH = 2 * GDN_QK + GDN_V
SWA_HEADS = 4
SWA_KV_HEADS = 2
SWA_GROUP = SWA_HEADS // SWA_KV_HEADS
SWA_HD = 64
WINDOW = 128
WIN_CHUNKS = WINDOW // CHUNK
SWA_Q = SWA_HEADS * SWA_HD
SWA_KV = SWA_KV_HEADS * SWA_HD
SC_WIDTH = 256
SC_CONV = 3
MEM_TOKENS = 256
MEM_HEADS = 4
MEM_HD = D_MODEL // MEM_HEADS
D_FF = 2816
FFN_CONV = 3
IN_SIZES = (GDN_CONV_CH, GDN_V, GDN_HEADS, GDN_HEADS, SWA_Q, SWA_KV, SWA_KV, SC_WIDTH, SC_WIDTH, SC_WIDTH)
D_IN = GDN_CONV_CH + GDN_V + 2 * GDN_HEADS + SWA_Q + 2 * SWA_KV + 3 * SC_WIDTH
MIX_WIDTH = GDN_V + SWA_Q + SC_WIDTH

kernel_name = "hymba_gdn_swa_shortconv_streaming_step"


def rmsnorm(x, g):
    xf = x.astype(jnp.float32)
    y = xf * lax.rsqrt(jnp.mean(xf * xf, axis=-1, keepdims=True) + EPS)
    return (y * g.astype(jnp.float32)).astype(x.dtype)


def l2norm(x):
    xf = x.astype(jnp.float32)
    return (xf * lax.rsqrt(jnp.sum(xf * xf, axis=-1, keepdims=True) + EPS)).astype(x.dtype)


def split_cols(a, sizes):
    out, o = [], 0
    for s in sizes:
        out.append(a[..., o:o + s])
        o += s
    return out


def causal_dwconv(x, buf, w):
    W = w.shape[0]
    T = x.shape[1]
    xp = jnp.concatenate([buf.astype(x.dtype), x], axis=1)
    y = xp[:, 0:T] * w[0]
    for i in range(1, W):
        y = y + xp[:, i:i + T] * w[i]
    return y, xp[:, T:]


def gated_delta(q, k, v, g, beta, S0):
    B, T, H, dk = q.shape
    dv = v.shape[-1]
    C = CHUNK if T % CHUNK == 0 else T
    N = T // C

    def blk(a):
        a = a.reshape((B, N, C, H) + a.shape[3:])
        return jnp.moveaxis(a, (1, 3), (0, 2))

    qf, kf, vf = (blk(a).astype(jnp.float32) for a in (q, k, v))
    gc = jnp.cumsum(blk(g).astype(jnp.float32), axis=-1)
    bt = blk(beta).astype(jnp.float32)
    idx = jnp.arange(C)
    strict = idx[:, None] > idx[None, :]
    incl = idx[:, None] >= idx[None, :]
    diff = gc[..., :, None] - gc[..., None, :]
    dec_strict = jnp.exp(jnp.where(strict, diff, NEG_INF))
    dec_incl = jnp.exp(jnp.where(incl, diff, NEG_INF))
    kk = jnp.einsum('nbhid,nbhjd->nbhij', kf, kf)
    lower = bt[..., :, None] * kk * dec_strict + jnp.eye(C, dtype=jnp.float32)
    rhs = jnp.concatenate([bt[..., None] * vf, (bt * jnp.exp(gc))[..., None] * kf], axis=-1)
    sol = lax.linalg.triangular_solve(lower, rhs, left_side=True, lower=True, unit_diagonal=True)
    u0, wk = sol[..., :dv], sol[..., dv:]
    qk = jnp.einsum('nbhid,nbhjd->nbhij', qf, kf) * dec_incl
    qg = qf * jnp.exp(gc)[..., None]
    kd = kf * jnp.exp(gc[..., -1:] - gc)[..., None]
    gl = jnp.exp(gc[..., -1])

    def step(S, xs):
        u0c, wc, qkc, qgc, kdc, glc = xs
        u = u0c - jnp.einsum('bhck,bhkv->bhcv', wc, S)
        o = jnp.einsum('bhck,bhkv->bhcv', qgc, S) + jnp.einsum('bhij,bhjv->bhiv', qkc, u)
        S = glc[..., None, None] * S + jnp.einsum('bhck,bhcv->bhkv', kdc, u)
        return S, o

    S, o = lax.scan(step, S0.astype(jnp.float32), (u0, wk, qk, qg, kd, gl))
    o = jnp.moveaxis(o, (0, 2), (1, 3)).reshape(B, T, H, dv)
    return o, S


def sink_probs(s, sinks):
    sk = sinks.astype(jnp.float32).reshape(SWA_KV_HEADS, SWA_GROUP)[:, :, None, None]
    m = jnp.maximum(jnp.max(s, axis=-1, keepdims=True), sk)
    p = jnp.exp(s - m)
    return p / (jnp.sum(p, axis=-1, keepdims=True) + jnp.exp(sk - m))


def swa_prompt(q, k, v, sinks):
    B, T = q.shape[:2]
    N = T // CHUNK
    qb = q.reshape(B, N, CHUNK, SWA_KV_HEADS, SWA_GROUP, SWA_HD)

    def band(a):
        ab = a.reshape(B, N, CHUNK, SWA_KV_HEADS, SWA_HD)
        ap = jnp.pad(ab, ((0, 0), (WIN_CHUNKS, 0), (0, 0), (0, 0), (0, 0)))
        return jnp.concatenate([ap[:, i:i + N] for i in range(WIN_CHUNKS + 1)], axis=2)

    kb, vb = band(k), band(v)
    key_chunk = jnp.arange(N)[:, None] + jnp.arange(WIN_CHUNKS + 1)[None, :] - WIN_CHUNKS
    valid = jnp.repeat(key_chunk >= 0, CHUNK, axis=1)
    s = jnp.einsum('bnqhgd,bnkhd->bnhgqk', qb, kb, preferred_element_type=jnp.float32) * (SWA_HD ** -0.5)
    s = jnp.where(valid[None, :, None, None, None, :], s, NEG_INF)
    p = sink_probs(s, sinks).astype(v.dtype)
    o = jnp.einsum('bnhgqk,bnkhd->bnqhgd', p, vb)
    return o.reshape(B, T, SWA_Q)


def swa_sample(q, k, v, k_past, v_past, sinks):
    B, T = q.shape[:2]
    qg = q.reshape(B, T, SWA_KV_HEADS, SWA_GROUP, SWA_HD)
    kk = jnp.concatenate([k_past.astype(k.dtype), k], axis=1)
    vv = jnp.concatenate([v_past.astype(v.dtype), v], axis=1)
    s = jnp.einsum('bqhgd,bkhd->bhgqk', qg, kk, preferred_element_type=jnp.float32) * (SWA_HD ** -0.5)
    p = sink_probs(s, sinks).astype(v.dtype)
    o = jnp.einsum('bhgqk,bkhd->bqhgd', p, vv)
    return o.reshape(B, T, SWA_Q)


def memory_kv(mem, lw):
    B, M, _ = mem.shape
    m = rmsnorm(mem, lw['mem_in_norm_g'])
    k = rmsnorm((m @ lw['w_mk']).reshape(B, M, MEM_HEADS, MEM_HD), lw['mem_k_norm_g'])
    v = (m @ lw['w_mv']).reshape(B, M, MEM_HEADS, MEM_HD)
    return k, v


def memory_attend(hn, mem_k, mem_v, lw):
    B, T, _ = hn.shape
    q = rmsnorm((hn @ lw['w_mq']).reshape(B, T, MEM_HEADS, MEM_HD), lw['mem_q_norm_g'])
    s = jnp.einsum('bqhd,bkhd->bhqk', q, mem_k.astype(q.dtype), preferred_element_type=jnp.float32) * (MEM_HD ** -0.5)
    p = jax.nn.softmax(s, axis=-1).astype(hn.dtype)
    o = jnp.einsum('bhqk,bkhd->bqhd', p, mem_v.astype(hn.dtype)).reshape(B, T, D_MODEL)
    return o @ lw['w_mo']


def trunk_layer(x, mem_k, mem_v, gdn_S0, gdn_buf, sc_buf, ffn_buf, swa_k_past, swa_v_past, lw):
    B, T, _ = x.shape
    xn = rmsnorm(x, lw['norm_mix_g'])
    (qkv_raw, z, a, b, sq, sk, sv, sc_b, sc_c, sc_h) = split_cols(xn @ lw['w_in'], IN_SIZES)
    qkv, gdn_buf_new = causal_dwconv(qkv_raw, gdn_buf, lw['w_gdn_conv'])
    qkv = jax.nn.silu(qkv)
    gq, gk, gv = split_cols(qkv, (GDN_QK, GDN_QK, GDN_V))
    gq = l2norm(gq.reshape(B, T, GDN_HEADS, GDN_DK)) * (GDN_DK ** -0.5)
    gk = l2norm(gk.reshape(B, T, GDN_HEADS, GDN_DK))
    gv = gv.reshape(B, T, GDN_HEADS, GDN_DV)
    log_decay = -jnp.exp(lw['gdn_a_log'].astype(jnp.float32)) * jax.nn.softplus(
        a.astype(jnp.float32) + lw['gdn_dt_bias'].astype(jnp.float32))
    beta = jax.nn.sigmoid(b.astype(jnp.float32))
    go, S = gated_delta(gq, gk, gv, log_decay, beta, gdn_S0)
    go = rmsnorm(go, lw['gdn_norm_g']) * jax.nn.silu(z.astype(jnp.float32).reshape(B, T, GDN_HEADS, GDN_DV))
    out_a = go.reshape(B, T, GDN_V).astype(x.dtype)
    sq = rmsnorm(sq.reshape(B, T, SWA_HEADS, SWA_HD), lw['swa_q_norm_g'])
    sk = rmsnorm(sk.reshape(B, T, SWA_KV_HEADS, SWA_HD), lw['swa_k_norm_g'])
    sv = sv.reshape(B, T, SWA_KV_HEADS, SWA_HD)
    if swa_k_past is None:
        out_b = swa_prompt(sq, sk, sv, lw['swa_sinks'])
    else:
        out_b = swa_sample(sq, sk, sv, swa_k_past, swa_v_past, lw['swa_sinks'])
    cu, sc_buf_new = causal_dwconv(sc_c * sc_h, sc_buf, lw['w_sc_conv'])
    out_c = sc_b * cu
    h = x + jnp.concatenate([out_a, out_b, out_c], axis=-1) @ lw['w_o']
    h = h + memory_attend(rmsnorm(h, lw['norm_mem_g']), mem_k, mem_v, lw)
    u, ffn_buf_new = causal_dwconv(rmsnorm(h, lw['norm_ffn_g']) @ lw['w_up'], ffn_buf, lw['w_ffn_conv'])
    ug, uv = split_cols(u, (D_FF, D_FF))
    h = h + (jax.nn.silu(ug) * uv) @ lw['w_down']
    return h, S.astype(x.dtype), gdn_buf_new, sc_buf_new, ffn_buf_new, sk, sv


def setup_inputs(seed: int = 0) -> dict:
    key = jax.random.key(seed)
    ks = iter(jax.random.split(key, 48))

    def nrm(shape, scale=1.0):
        return jax.random.normal(next(ks), shape, jnp.float32) * scale

    def dense(shape):
        return nrm(shape, shape[-2] ** -0.5)

    def gain(shape):
        return 1.0 + nrm(shape, 0.02)

    swa_buf = min(WINDOW, PAST_LEN)
    dt = jnp.exp(jax.random.uniform(next(ks), (DEPTH, GDN_HEADS), jnp.float32, np.log(1e-3), np.log(1e-1)))
    inputs = {
        'x_prompt': nrm((BATCH, SEQ, D_MODEL)),
        'x_sample': nrm((DEC_BATCH, DEC_SEQ, D_MODEL)),
        'mem_prompt': nrm((BATCH, MEM_TOKENS, D_MODEL)),
        'state_gdn': nrm((DEPTH, DEC_BATCH, GDN_HEADS, GDN_DK, GDN_DV), 0.1),
        'state_gdn_conv': nrm((DEPTH, DEC_BATCH, GDN_CONV - 1, GDN_CONV_CH)),
        'cache_swa_k': nrm((DEPTH, DEC_BATCH, swa_buf, SWA_KV_HEADS, SWA_HD)),
        'cache_swa_v': nrm((DEPTH, DEC_BATCH, swa_buf, SWA_KV_HEADS, SWA_HD)),
        'state_sc_conv': nrm((DEPTH, DEC_BATCH, SC_CONV - 1, SC_WIDTH)),
        'cache_mem_k': nrm((DEPTH, DEC_BATCH, MEM_TOKENS, MEM_HEADS, MEM_HD)),
        'cache_mem_v': nrm((DEPTH, DEC_BATCH, MEM_TOKENS, MEM_HEADS, MEM_HD)),
        'state_ffn_conv': nrm((DEPTH, DEC_BATCH, FFN_CONV - 1, 2 * D_FF)),
        'norm_mix_g': gain((DEPTH, D_MODEL)),
        'w_in': dense((DEPTH, D_MODEL, D_IN)),
        'w_gdn_conv': nrm((DEPTH, GDN_CONV, GDN_CONV_CH), GDN_CONV ** -0.5),
        'gdn_a_log': jnp.log(jax.random.uniform(next(ks), (DEPTH, GDN_HEADS), jnp.float32, 1.0, 16.0)),
        'gdn_dt_bias': dt + jnp.log(-jnp.expm1(-dt)),
        'gdn_norm_g': gain((DEPTH, GDN_DV)),
        'swa_q_norm_g': gain((DEPTH, SWA_HD)),
        'swa_k_norm_g': gain((DEPTH, SWA_HD)),
        'swa_sinks': nrm((DEPTH, SWA_HEADS), 0.5),
        'w_sc_conv': nrm((DEPTH, SC_CONV, SC_WIDTH), SC_CONV ** -0.5),
        'w_o': dense((DEPTH, MIX_WIDTH, D_MODEL)),
        'norm_mem_g': gain((DEPTH, D_MODEL)),
        'mem_in_norm_g': gain((DEPTH, D_MODEL)),
        'w_mq': dense((DEPTH, D_MODEL, D_MODEL)),
        'w_mk': dense((DEPTH, D_MODEL, D_MODEL)),
        'w_mv': dense((DEPTH, D_MODEL, D_MODEL)),
        'mem_q_norm_g': gain((DEPTH, MEM_HD)),
        'mem_k_norm_g': gain((DEPTH, MEM_HD)),
        'w_mo': dense((DEPTH, D_MODEL, D_MODEL)),
        'norm_ffn_g': gain((DEPTH, D_MODEL)),
        'w_up': dense((DEPTH, D_MODEL, 2 * D_FF)),
        'w_ffn_conv': nrm((DEPTH, FFN_CONV, 2 * D_FF), FFN_CONV ** -0.5),
        'w_down': dense((DEPTH, D_FF, D_MODEL)),
    }
    return inputs


def reference(x_prompt, x_sample, mem_prompt,
              state_gdn, state_gdn_conv, cache_swa_k, cache_swa_v, state_sc_conv,
              cache_mem_k, cache_mem_v, state_ffn_conv,
              norm_mix_g, w_in, w_gdn_conv, gdn_a_log, gdn_dt_bias, gdn_norm_g,
              swa_q_norm_g, swa_k_norm_g, swa_sinks, w_sc_conv, w_o,
              norm_mem_g, mem_in_norm_g, w_mq, w_mk, w_mv, mem_q_norm_g, mem_k_norm_g, w_mo,
              norm_ffn_g, w_up, w_ffn_conv, w_down):
    hp, hs = x_prompt, x_sample
    Bp, Tp, _ = x_prompt.shape
    keep = min(WINDOW, Tp)
    p_S, p_gb, p_sk, p_sv, p_sb, p_mk, p_mv, p_fb = [], [], [], [], [], [], [], []
    s_S, s_gb, s_sk, s_sv, s_sb, s_fb = [], [], [], [], [], []
    for l in range(DEPTH):
        lw = {
            'norm_mix_g': norm_mix_g[l], 'w_in': w_in[l], 'w_gdn_conv': w_gdn_conv[l],
            'gdn_a_log': gdn_a_log[l], 'gdn_dt_bias': gdn_dt_bias[l], 'gdn_norm_g': gdn_norm_g[l],
            'swa_q_norm_g': swa_q_norm_g[l], 'swa_k_norm_g': swa_k_norm_g[l], 'swa_sinks': swa_sinks[l],
            'w_sc_conv': w_sc_conv[l], 'w_o': w_o[l], 'norm_mem_g': norm_mem_g[l],
            'mem_in_norm_g': mem_in_norm_g[l], 'w_mq': w_mq[l], 'w_mk': w_mk[l], 'w_mv': w_mv[l],
            'mem_q_norm_g': mem_q_norm_g[l], 'mem_k_norm_g': mem_k_norm_g[l], 'w_mo': w_mo[l],
            'norm_ffn_g': norm_ffn_g[l], 'w_up': w_up[l], 'w_ffn_conv': w_ffn_conv[l], 'w_down': w_down[l],
        }
        mk, mv = memory_kv(mem_prompt, lw)
        dt_p = hp.dtype
        hp, S, gb, sb, fb, sk, sv = trunk_layer(
            hp, mk, mv,
            jnp.zeros((Bp, GDN_HEADS, GDN_DK, GDN_DV), dt_p),
            jnp.zeros((Bp, GDN_CONV - 1, GDN_CONV_CH), dt_p),
            jnp.zeros((Bp, SC_CONV - 1, SC_WIDTH), dt_p),
            jnp.zeros((Bp, FFN_CONV - 1, 2 * D_FF), dt_p),
            None, None, lw)
        p_S.append(S); p_gb.append(gb); p_sb.append(sb); p_fb.append(fb)
        p_sk.append(sk[:, Tp - keep:]); p_sv.append(sv[:, Tp - keep:])
        p_mk.append(mk); p_mv.append(mv)
        hs, S, gb, sb, fb, sk, sv = trunk_layer(
            hs, cache_mem_k[l], cache_mem_v[l], state_gdn[l], state_gdn_conv[l],
            state_sc_conv[l], state_ffn_conv[l], cache_swa_k[l], cache_swa_v[l], lw)
        s_S.append(S); s_gb.append(gb); s_sb.append(sb); s_fb.append(fb)
        s_sk.append(sk); s_sv.append(sv)
    y_prompt, y_sample = hp, hs
    p_state_gdn = jnp.stack(p_S)
    p_state_gdn_conv = jnp.stack(p_gb)
    p_cache_swa_k = jnp.stack(p_sk)
    p_cache_swa_v = jnp.stack(p_sv)
    p_state_sc_conv = jnp.stack(p_sb)
    p_cache_mem_k = jnp.stack(p_mk)
    p_cache_mem_v = jnp.stack(p_mv)
    p_state_ffn_conv = jnp.stack(p_fb)
    s_state_gdn = jnp.stack(s_S)
    s_state_gdn_conv = jnp.stack(s_gb)
    s_swa_k_new = jnp.stack(s_sk)
    s_swa_v_new = jnp.stack(s_sv)
    s_state_sc_conv = jnp.stack(s_sb)
    s_state_ffn_conv = jnp.stack(s_fb)
    return (y_prompt, y_sample,
            p_state_gdn, p_state_gdn_conv, p_cache_swa_k, p_cache_swa_v, p_state_sc_conv,
            p_cache_mem_k, p_cache_mem_v, p_state_ffn_conv,
            s_state_gdn, s_state_gdn_conv, s_swa_k_new, s_swa_v_new, s_state_sc_conv, s_state_ffn_conv)
```

```python
import functools

import jax
import jax.numpy as jnp
from jax import lax
from jax.experimental import pallas as pl
from jax.experimental.pallas import tpu as pltpu

F32 = jnp.float32
BF16 = jnp.bfloat16

D_MODEL = 1024
CHUNK = 64
EPS = 1e-6
NEG_INF = -1e30
GDN_HEADS = 4
GDN_DK = 128
GDN_DV = 128
GDN_CONV = 4
GDN_QK = GDN_HEADS * GDN_DK
GDN_V = GDN_HEADS * GDN_DV
GDN_CONV_CH = 2 * GDN_QK + GDN_V
SWA_HEADS = 4
SWA_KV_HEADS = 2
SWA_GROUP = SWA_HEADS // SWA_KV_HEADS
SWA_HD = 64
WINDOW = 128
WIN_CHUNKS = WINDOW // CHUNK
SWA_Q = SWA_HEADS * SWA_HD
SWA_KV = SWA_KV_HEADS * SWA_HD
SWA_W = SWA_Q + 2 * SWA_KV
SC_WIDTH = 256
SC_CONV = 3
MEM_TOKENS = 256
MEM_HEADS = 4
MEM_HD = D_MODEL // MEM_HEADS
D_FF = 2816
FFN_CONV = 3
LANES = 128
SUBLANES = 8
AB_W = LANES
SC_W = 3 * SC_WIDTH
IN_COLS = (
    (0, GDN_CONV_CH),
    (GDN_CONV_CH, GDN_CONV_CH + GDN_V),
    (GDN_CONV_CH + GDN_V, GDN_CONV_CH + GDN_V + SWA_W),
    (GDN_CONV_CH + GDN_V + SWA_W, GDN_CONV_CH + GDN_V + SWA_W + SC_W),
    (GDN_CONV_CH + GDN_V + SWA_W + SC_W, GDN_CONV_CH + GDN_V + SWA_W + SC_W + AB_W),
)
D_IN_PAD = IN_COLS[-1][1]
FFN_COL_CHUNK = 256
VMEM_LIMIT = 56 * 1024 * 1024


def _dot(a, b):
    return jnp.dot(a.astype(BF16), b.astype(BF16), preferred_element_type=F32)


def _dot_nt(a, b):
    return lax.dot_general(a.astype(BF16), b.astype(BF16), (((1,), (1,)), ((), ())),
                           preferred_element_type=F32)


def _dot_tn(a, b):
    return lax.dot_general(a.astype(BF16), b.astype(BF16), (((0,), (0,)), ((), ())),
                           preferred_element_type=F32)


def _rms(x, g):
    return x * lax.rsqrt(jnp.mean(x * x, axis=-1, keepdims=True) + EPS) * g


def _silu(x):
    return x * jax.nn.sigmoid(x)


def _const_spec(shape):
    nd = len(shape)
    return pl.BlockSpec(shape, lambda *_: (0,) * nd)


def _params(sem):
    return pltpu.CompilerParams(dimension_semantics=sem, vmem_limit_bytes=VMEM_LIMIT)


def _in_proj_kernel(x_ref, g_ref, w_ref, qkv_ref, z_ref, swa_ref, sc_ref, ab_ref):
    xn = _rms(x_ref[...], g_ref[...]).astype(BF16)
    for o_ref, (lo, hi) in zip((qkv_ref, z_ref, swa_ref, sc_ref, ab_ref), IN_COLS):
        o_ref[...] = jnp.dot(xn, w_ref[:, lo:hi], preferred_element_type=F32)


def _in_proj(x2, g, w, tm):
    n = x2.shape[0]
    widths = [hi - lo for lo, hi in IN_COLS]
    return pl.pallas_call(
        _in_proj_kernel,
        grid=(n // tm,),
        in_specs=[pl.BlockSpec((tm, D_MODEL), lambda i: (i, 0)),
                  _const_spec((1, D_MODEL)),
                  _const_spec((D_MODEL, D_IN_PAD))],
        out_specs=[pl.BlockSpec((tm, wd), lambda i: (i, 0)) for wd in widths],
        out_shape=[jax.ShapeDtypeStruct((n, wd), F32) for wd in widths],
        compiler_params=_params(("arbitrary",)),
        name="in_proj",
    )(x2, g, w)


def _cumsum_rows(x):
    rows = x.shape[0]
    row = lax.broadcasted_iota(jnp.int32, x.shape, 0)
    s = 1
    while s < rows:
        x = x + jnp.where(row >= s, pltpu.roll(x, s, axis=0), 0.0)
        s *= 2
    return x


def _gdn_kernel(qkv_ref, ab_ref, z_ref, sc_ref, wconv_ref, wsc_ref, hp_ref, gn_ref,
                s0_ref, gbuf_ref, scbuf_ref,
                oa_ref, oc_ref, sout_ref, scout_ref,
                s_scr, xp_scr, scp_scr, *, bb, C):
    n = pl.program_id(1)
    pad = SUBLANES

    @pl.when(n == 0)
    def _init():
        s_scr[...] = s0_ref[...]
        xp_scr[:, pad - (GDN_CONV - 1):pad, :] = gbuf_ref[...]
        scp_scr[:, pad - (SC_CONV - 1):pad, :] = scbuf_ref[...]

    row_i = lax.broadcasted_iota(jnp.int32, (C, C), 0)
    col_j = lax.broadcasted_iota(jnp.int32, (C, C), 1)
    incl = row_i >= col_j
    strict = row_i > col_j
    neg_a = -jnp.exp(hp_ref[0:1, :])
    dt_b = hp_ref[1:2, :]
    n_sq = C.bit_length() - 2

    for b in range(bb):
        x = qkv_ref[b]
        xp_scr[b, pad:pad + C, :] = x
        y = xp_scr[b, pad - 3:pad - 3 + C, :] * wconv_ref[0:1, :]
        y = y + xp_scr[b, pad - 2:pad - 2 + C, :] * wconv_ref[1:2, :]
        y = y + xp_scr[b, pad - 1:pad - 1 + C, :] * wconv_ref[2:3, :]
        y = y + x * wconv_ref[3:4, :]
        xp_scr[b, pad - 3:pad, :] = x[C - 3:C, :]
        y = _silu(y)

        ab = ab_ref[b]
        t = ab + dt_b
        softplus = jnp.maximum(t, 0.0) + jnp.log1p(jnp.exp(-jnp.abs(t)))
        gc = _cumsum_rows(neg_a * softplus)
        beta = jax.nn.sigmoid(ab)
        gct = jnp.transpose(jnp.concatenate(
            [gc, jnp.zeros((LANES - C, LANES), F32)], axis=0))
        zb = z_ref[b]

        for h in range(GDN_HEADS):
            q = y[:, h * GDN_DK:(h + 1) * GDN_DK]
            k = y[:, GDN_QK + h * GDN_DK:GDN_QK + (h + 1) * GDN_DK]
            v = y[:, 2 * GDN_QK + h * GDN_DV:2 * GDN_QK + (h + 1) * GDN_DV]
            q = q * lax.rsqrt(jnp.sum(q * q, axis=-1, keepdims=True) + EPS) * (GDN_DK ** -0.5)
            k = k * lax.rsqrt(jnp.sum(k * k, axis=-1, keepdims=True) + EPS)
            bt = beta[:, GDN_HEADS + h:GDN_HEADS + h + 1]
            gci = gc[:, h:h + 1]
            gcj = gct[h:h + 1, 0:C]
            g_last = gc[C - 1:C, h:h + 1]
            e_incl = jnp.exp(jnp.where(incl, gci - gcj, NEG_INF))
            e_strict = jnp.where(strict, e_incl, 0.0)
            eg = jnp.exp(gci)

            kb = k.astype(BF16)
            kk = _dot_nt(kb, kb)
            p_pow = -(bt * kk * e_strict)
            n_acc = p_pow
            for _ in range(n_sq):
                pb = p_pow.astype(BF16)
                p_pow = _dot(pb, pb)
                n_acc = n_acc + p_pow + _dot(p_pow, n_acc)
            rhs = jnp.concatenate([bt * v, (bt * eg) * k], axis=-1)
            sol = rhs + _dot(n_acc, rhs)
            u0 = sol[:, :GDN_DV]
            wk = sol[:, GDN_DV:]
            qk = _dot_nt(q, kb) * e_incl
            qg = q * eg
            kd = k * jnp.exp(g_last - gci)
            gl = jnp.exp(g_last)

            s_prev = s_scr[b, h]
            sb = s_prev.astype(BF16)
            u = u0 - _dot(wk, sb)
            o = _dot(qg, sb) + _dot(qk, u)
            s_scr[b, h] = gl * s_prev + _dot_tn(kd, u)

            on = _rms(o, gn_ref[...]) * _silu(zb[:, h * GDN_DV:(h + 1) * GDN_DV])
            oa_ref[b, :, h * GDN_DV:(h + 1) * GDN_DV] = on

        scv = sc_ref[b]
        prod = scv[:, SC_WIDTH:2 * SC_WIDTH] * scv[:, 2 * SC_WIDTH:3 * SC_WIDTH]
        scp_scr[b, pad:pad + C, :] = prod
        cu = scp_scr[b, pad - 2:pad - 2 + C, :] * wsc_ref[0:1, :]
        cu = cu + scp_scr[b, pad - 1:pad - 1 + C, :] * wsc_ref[1:2, :]
        cu = cu + prod * wsc_ref[2:3, :]
        scp_scr[b, pad - 2:pad, :] = prod[C - 2:C, :]
        oc_ref[b] = scv[:, 0:SC_WIDTH] * cu

    @pl.when(n == pl.num_programs(1) - 1)
    def _fin():
        sout_ref[...] = s_scr[...]
        scout_ref[...] = scp_scr[:, pad - (SC_CONV - 1):pad, :]


def _gdn(qkv, ab, z, sc, wconv, wsc, hp, gn, s0, gbuf, scbuf, C, bb):
    B, T, _ = qkv.shape
    N = T // C
    tok = lambda w: pl.BlockSpec((bb, C, w), lambda i, n: (i, n, 0))
    per_b = lambda *s: pl.BlockSpec((bb,) + s, lambda i, n: (i,) + (0,) * len(s))
    return pl.pallas_call(
        functools.partial(_gdn_kernel, bb=bb, C=C),
        grid=(B // bb, N),
        in_specs=[tok(GDN_CONV_CH), tok(AB_W), tok(GDN_V), tok(SC_W),
                  _const_spec((GDN_CONV, GDN_CONV_CH)), _const_spec((SC_CONV, SC_WIDTH)),
                  _const_spec((SUBLANES, LANES)), _const_spec((1, GDN_DV)),
                  per_b(GDN_HEADS, GDN_DK, GDN_DV), per_b(GDN_CONV - 1, GDN_CONV_CH),
                  per_b(SC_CONV - 1, SC_WIDTH)],
        out_specs=[tok(GDN_V), tok(SC_WIDTH),
                   per_b(GDN_HEADS, GDN_DK, GDN_DV), per_b(SC_CONV - 1, SC_WIDTH)],
        out_shape=[jax.ShapeDtypeStruct((B, T, GDN_V), F32),
                   jax.ShapeDtypeStruct((B, T, SC_WIDTH), F32),
                   jax.ShapeDtypeStruct((B, GDN_HEADS, GDN_DK, GDN_DV), F32),
                   jax.ShapeDtypeStruct((B, SC_CONV - 1, SC_WIDTH), F32)],
        scratch_shapes=[pltpu.VMEM((bb, GDN_HEADS, GDN_DK, GDN_DV), F32),
                        pltpu.VMEM((bb, SUBLANES + C, GDN_CONV_CH), F32),
                        pltpu.VMEM((bb, SUBLANES + C, SC_WIDTH), F32)],
        compiler_params=_params(("arbitrary", "arbitrary")),
        name="gdn",
    )(qkv, ab, z, sc, wconv, wsc, hp, gn, s0, gbuf, scbuf)


def _swa_norm_k(x, gk):
    return jnp.concatenate(
        [_rms(x[:, SWA_Q + i * SWA_HD:SWA_Q + (i + 1) * SWA_HD], gk) for i in range(SWA_KV_HEADS)],
        axis=-1)


def _swa_heads(x, k_all, v_all, valid, gq, sinks):
    outs = []
    for hq in range(SWA_HEADS):
        kv = hq // SWA_GROUP
        q = _rms(x[:, hq * SWA_HD:(hq + 1) * SWA_HD], gq)
        s = _dot_nt(q, k_all[:, kv * SWA_HD:(kv + 1) * SWA_HD]) * (SWA_HD ** -0.5)
        if valid is not None:
            s = jnp.where(valid, s, NEG_INF)
        sink = sinks[:, hq:hq + 1]
        m = jnp.maximum(jnp.max(s, axis=-1, keepdims=True), sink)
        p = jnp.exp(s - m)
        p = p / (jnp.sum(p, axis=-1, keepdims=True) + jnp.exp(sink - m))
        outs.append(_dot(p, v_all[:, kv * SWA_HD:(kv + 1) * SWA_HD]))
    return jnp.concatenate(outs, axis=-1)


def _swa_prompt_kernel(x_ref, gq_ref, gk_ref, sink_ref, o_ref, kn_ref, k_scr, v_scr):
    c = pl.program_id(1)
    band = (WIN_CHUNKS + 1) * CHUNK

    @pl.when(c == 0)
    def _init():
        k_scr[...] = jnp.zeros_like(k_scr)
        v_scr[...] = jnp.zeros_like(v_scr)

    k_old = k_scr[CHUNK:band, :]
    v_old = v_scr[CHUNK:band, :]
    k_scr[0:band - CHUNK, :] = k_old
    v_scr[0:band - CHUNK, :] = v_old
    x = x_ref[...]
    kn = _swa_norm_k(x, gk_ref[...])
    k_scr[band - CHUNK:band, :] = kn
    v_scr[band - CHUNK:band, :] = x[:, SWA_Q + SWA_KV:SWA_W]
    kn_ref[...] = kn
    key_chunk = lax.broadcasted_iota(jnp.int32, (1, band), 1) // CHUNK + (c - WIN_CHUNKS)
    o_ref[...] = _swa_heads(x, k_scr[...], v_scr[...], key_chunk >= 0, gq_ref[...], sink_ref[...])


def _swa_prompt(swa, gq, gk, sinks):
    B, T, _ = swa.shape
    band = (WIN_CHUNKS + 1) * CHUNK
    return pl.pallas_call(
        _swa_prompt_kernel,
        grid=(B, T // CHUNK),
        in_specs=[pl.BlockSpec((None, CHUNK, SWA_W), lambda b, c: (b, c, 0)),
                  _const_spec((1, SWA_HD)), _const_spec((1, SWA_HD)), _const_spec((1, LANES))],
        out_specs=[pl.BlockSpec((None, CHUNK, SWA_Q), lambda b, c: (b, c, 0)),
                   pl.BlockSpec((None, CHUNK, SWA_KV), lambda b, c: (b, c, 0))],
        out_shape=[jax.ShapeDtypeStruct((B, T, SWA_Q), F32),
                   jax.ShapeDtypeStruct((B, T, SWA_KV), F32)],
        scratch_shapes=[pltpu.VMEM((band, SWA_KV), F32), pltpu.VMEM((band, SWA_KV), F32)],
        compiler_params=_params(("arbitrary", "arbitrary")),
        name="swa_prompt",
    )(swa, gq, gk, sinks)


def _swa_sample_kernel(x_ref, kp_ref, vp_ref, gq_ref, gk_ref, sink_ref, o_ref, kn_ref,
                       k_scr, v_scr, *, past, T):
    x = x_ref[...]
    kn = _swa_norm_k(x, gk_ref[...])
    k_scr[0:past, :] = kp_ref[...]
    v_scr[0:past, :] = vp_ref[...]
    k_scr[past:past + T, :] = kn
    v_scr[past:past + T, :] = x[:, SWA_Q + SWA_KV:SWA_W]
    kn_ref[...] = kn
    o_ref[...] = _swa_heads(x, k_scr[...], v_scr[...], None, gq_ref[...], sink_ref[...])


def _swa_sample(swa, k_past, v_past, gq, gk, sinks):
    B, T, _ = swa.shape
    past = k_past.shape[1]
    return pl.pallas_call(
        functools.partial(_swa_sample_kernel, past=past, T=T),
        grid=(B,),
        in_specs=[pl.BlockSpec((None, T, SWA_W), lambda b: (b, 0, 0)),
                  pl.BlockSpec((None, past, SWA_KV), lambda b: (b, 0, 0)),
                  pl.BlockSpec((None, past, SWA_KV), lambda b: (b, 0, 0)),
                  _const_spec((1, SWA_HD)), _const_spec((1, SWA_HD)), _const_spec((1, LANES))],
        out_specs=[pl.BlockSpec((None, T, SWA_Q), lambda b: (b, 0, 0)),
                   pl.BlockSpec((None, T, SWA_KV), lambda b: (b, 0, 0))],
        out_shape=[jax.ShapeDtypeStruct((B, T, SWA_Q), F32),
                   jax.ShapeDtypeStruct((B, T, SWA_KV), F32)],
        scratch_shapes=[pltpu.VMEM((past + T, SWA_KV), F32), pltpu.VMEM((past + T, SWA_KV), F32)],
        compiler_params=_params(("arbitrary",)),
        name="swa_sample",
    )(swa, k_past, v_past, gq, gk, sinks)


def _mem_kv_kernel(m_ref, g_ref, wk_ref, wv_ref, gk_ref, k_ref, v_ref):
    m = _rms(m_ref[...], g_ref[...]).astype(BF16)
    k = jnp.dot(m, wk_ref[...], preferred_element_type=F32)
    k_ref[...] = jnp.concatenate(
        [_rms(k[:, h * MEM_HD:(h + 1) * MEM_HD], gk_ref[...]) for h in range(MEM_HEADS)], axis=-1)
    v_ref[...] = jnp.dot(m, wv_ref[...], preferred_element_type=F32)


def _mem_kv(mem, g, wk, wv, gk):
    B, M, _ = mem.shape
    blk = pl.BlockSpec((None, M, D_MODEL), lambda b: (b, 0, 0))
    return pl.pallas_call(
        _mem_kv_kernel,
        grid=(B,),
        in_specs=[blk, _const_spec((1, D_MODEL)), _const_spec((D_MODEL, D_MODEL)),
                  _const_spec((D_MODEL, D_MODEL)), _const_spec((1, MEM_HD))],
        out_specs=[blk, blk],
        out_shape=[jax.ShapeDtypeStruct((B, M, D_MODEL), F32)] * 2,
        compiler_params=_params(("arbitrary",)),
        name="mem_kv",
    )(mem, g, wk, wv, gk)


def _post_kernel(x_ref, a_ref, b_ref, c_ref, wo_ref, gm_ref, wq_ref, gq_ref, mk_ref, mv_ref,
                 wmo_ref, o_ref):
    h = x_ref[...]
    h = h + (jnp.dot(a_ref[...].astype(BF16), wo_ref[0:GDN_V, :], preferred_element_type=F32)
             + jnp.dot(b_ref[...].astype(BF16), wo_ref[GDN_V:GDN_V + SWA_Q, :],
                       preferred_element_type=F32)
             + jnp.dot(c_ref[...].astype(BF16), wo_ref[GDN_V + SWA_Q:, :],
                       preferred_element_type=F32))
    hn = _rms(h, gm_ref[...]).astype(BF16)
    q = jnp.dot(hn, wq_ref[...], preferred_element_type=F32)
    outs = []
    for hd in range(MEM_HEADS):
        sl = slice(hd * MEM_HD, (hd + 1) * MEM_HD)
        qh = _rms(q[:, sl], gq_ref[...])
        s = _dot_nt(qh, mk_ref[:, sl]) * (MEM_HD ** -0.5)
        m = jnp.max(s, axis=-1, keepdims=True)
        p = jnp.exp(s - m)
        p = p / jnp.sum(p, axis=-1, keepdims=True)
        outs.append(_dot(p, mv_ref[:, sl]))
    o = jnp.concatenate(outs, axis=-1).astype(BF16)
    o_ref[...] = h + jnp.dot(o, wmo_ref[...], preferred_element_type=F32)


def _post(x, oa, ob, oc, wo, gm, wq, gq, mk, mv, wmo, tm):
    B, T, _ = x.shape
    tok = lambda w: pl.BlockSpec((None, tm, w), lambda b, i: (b, i, 0))
    mem = pl.BlockSpec((None, MEM_TOKENS, D_MODEL), lambda b, i: (b, 0, 0))
    sq = _const_spec((D_MODEL, D_MODEL))
    return pl.pallas_call(
        _post_kernel,
        grid=(B, T // tm),
        in_specs=[tok(D_MODEL), tok(GDN_V), tok(SWA_Q), tok(SC_WIDTH), sq,
                  _const_spec((1, D_MODEL)), sq, _const_spec((1, MEM_HD)), mem, mem, sq],
        out_specs=tok(D_MODEL),
        out_shape=jax.ShapeDtypeStruct((B, T, D_MODEL), F32),
        compiler_params=_params(("arbitrary", "arbitrary")),
        name="post",
    )(x, oa, ob, oc, wo, gm, wq, gq, mk, mv, wmo)


def _ffn_kernel(h_ref, g_ref, wup_ref, wc_ref, wdn_ref, buf_ref, o_ref, st_ref,
                carry_scr, u_scr, *, tm):
    i = pl.program_id(1)
    pad = SUBLANES
    cw = FFN_COL_CHUNK

    @pl.when(i == 0)
    def _init():
        carry_scr[pad - (FFN_CONV - 1):pad, :] = buf_ref[...]

    h = h_ref[...]
    xn = _rms(h, g_ref[...]).astype(BF16)
    acc = h
    for j in range(D_FF // cw):
        conv = []
        for base in (0, D_FF):
            lo = base + j * cw
            u = jnp.dot(xn, wup_ref[:, lo:lo + cw], preferred_element_type=F32)
            u_scr[pad - 2:pad, :] = carry_scr[pad - 2:pad, lo:lo + cw]
            u_scr[pad:pad + tm, :] = u
            y = u_scr[pad - 2:pad - 2 + tm, :] * wc_ref[0:1, lo:lo + cw]
            y = y + u_scr[pad - 1:pad - 1 + tm, :] * wc_ref[1:2, lo:lo + cw]
            y = y + u * wc_ref[2:3, lo:lo + cw]
            carry_scr[pad - 2:pad, lo:lo + cw] = u[tm - 2:tm, :]
            conv.append(y)
        act = (_silu(conv[0]) * conv[1]).astype(BF16)
        acc = acc + jnp.dot(act, wdn_ref[j * cw:(j + 1) * cw, :], preferred_element_type=F32)
    o_ref[...] = acc

    @pl.when(i == pl.num_programs(1) - 1)
    def _fin():
        st_ref[...] = carry_scr[pad - (FFN_CONV - 1):pad, :]


def _ffn(h, g, wup, wc, wdn, buf, tm):
    B, T, _ = h.shape
    tok = pl.BlockSpec((None, tm, D_MODEL), lambda b, i: (b, i, 0))
    st = pl.BlockSpec((None, FFN_CONV - 1, 2 * D_FF), lambda b, i: (b, 0, 0))
    return pl.pallas_call(
        functools.partial(_ffn_kernel, tm=tm),
        grid=(B, T // tm),
        in_specs=[tok, _const_spec((1, D_MODEL)), _const_spec((D_MODEL, 2 * D_FF)),
                  _const_spec((FFN_CONV, 2 * D_FF)), _const_spec((D_FF, D_MODEL)), st],
        out_specs=[tok, st],
        out_shape=[jax.ShapeDtypeStruct((B, T, D_MODEL), F32),
                   jax.ShapeDtypeStruct((B, FFN_CONV - 1, 2 * D_FF), F32)],
        scratch_shapes=[pltpu.VMEM((SUBLANES, 2 * D_FF), F32),
                        pltpu.VMEM((SUBLANES + tm, FFN_COL_CHUNK), F32)],
        compiler_params=_params(("arbitrary", "arbitrary")),
        name="ffn",
    )(h, g, wup, wc, wdn, buf)


def _layer(x, mem_k, mem_v, s0, gbuf, scbuf, fbuf, swa_past, lw, tm_tok, tm_seq):
    B, T, _ = x.shape
    C = CHUNK if T % CHUNK == 0 else T
    qkv, z, swa, sc, ab = _in_proj(x.reshape(B * T, D_MODEL), lw['norm_mix_g'], lw['w_in'], tm_tok)
    r3 = lambda a: a.reshape(B, T, a.shape[-1])
    qkv, z, swa, sc, ab = r3(qkv), r3(z), r3(swa), r3(sc), r3(ab)
    oa, oc, s_new, sc_new = _gdn(qkv, ab, z, sc, lw['w_gdn_conv'], lw['w_sc_conv'], lw['hp'],
                                 lw['gdn_norm_g'], s0, gbuf, scbuf, C, 2)
    if swa_past is None:
        ob, kn = _swa_prompt(swa, lw['swa_q_norm_g'], lw['swa_k_norm_g'], lw['swa_sinks'])
    else:
        ob, kn = _swa_sample(swa, swa_past[0], swa_past[1], lw['swa_q_norm_g'], lw['swa_k_norm_g'],
                             lw['swa_sinks'])
    h = _post(x, oa, ob, oc, lw['w_o'], lw['norm_mem_g'], lw['w_mq'], lw['mem_q_norm_g'],
              mem_k, mem_v, lw['w_mo'], tm_seq)
    h, f_new = _ffn(h, lw['norm_ffn_g'], lw['w_up'], lw['w_ffn_conv'], lw['w_down'], fbuf, tm_seq)
    gb_new = qkv[:, T - (GDN_CONV - 1):, :]
    sk = kn.reshape(B, T, SWA_KV_HEADS, SWA_HD)
    sv = swa[:, :, SWA_Q + SWA_KV:].reshape(B, T, SWA_KV_HEADS, SWA_HD)
    return h, s_new, gb_new, sc_new, f_new, sk, sv


def kernel(x_prompt, x_sample, mem_prompt, state_gdn, state_gdn_conv, cache_swa_k, cache_swa_v, state_sc_conv, cache_mem_k, cache_mem_v, state_ffn_conv, norm_mix_g, w_in, w_gdn_conv, gdn_a_log, gdn_dt_bias, gdn_norm_g, swa_q_norm_g, swa_k_norm_g, swa_sinks, w_sc_conv, w_o, norm_mem_g, mem_in_norm_g, w_mq, w_mk, w_mv, mem_q_norm_g, mem_k_norm_g, w_mo, norm_ffn_g, w_up, w_ffn_conv, w_down):
    depth = w_in.shape[0]
    Bp, Tp, _ = x_prompt.shape
    Bs, Ts, _ = x_sample.shape
    keep = min(WINDOW, Tp)

    qz = GDN_CONV_CH + GDN_V
    w_in_r = jnp.concatenate(
        [w_in[:, :, :qz], w_in[:, :, qz + 2 * GDN_HEADS:], w_in[:, :, qz:qz + 2 * GDN_HEADS],
         jnp.zeros((depth, D_MODEL, AB_W - 2 * GDN_HEADS), w_in.dtype)], axis=-1).astype(BF16)
    hp = jnp.zeros((depth, SUBLANES, LANES), F32)
    hp = hp.at[:, 0, :GDN_HEADS].set(gdn_a_log).at[:, 1, :GDN_HEADS].set(gdn_dt_bias)
    sinks = jnp.zeros((depth, 1, LANES), F32).at[:, 0, :SWA_HEADS].set(swa_sinks)
    row = lambda a: a[:, None, :]
    stacked = {
        'norm_mix_g': row(norm_mix_g), 'w_in': w_in_r, 'w_gdn_conv': w_gdn_conv, 'hp': hp,
        'gdn_norm_g': row(gdn_norm_g), 'swa_q_norm_g': row(swa_q_norm_g),
        'swa_k_norm_g': row(swa_k_norm_g), 'swa_sinks': sinks, 'w_sc_conv': w_sc_conv,
        'w_o': w_o.astype(BF16), 'norm_mem_g': row(norm_mem_g), 'mem_in_norm_g': row(mem_in_norm_g),
        'w_mq': w_mq.astype(BF16), 'w_mk': w_mk.astype(BF16), 'w_mv': w_mv.astype(BF16),
        'mem_q_norm_g': row(mem_q_norm_g), 'mem_k_norm_g': row(mem_k_norm_g),
        'w_mo': w_mo.astype(BF16), 'norm_ffn_g': row(norm_ffn_g), 'w_up': w_up.astype(BF16),
        'w_ffn_conv': w_ffn_conv, 'w_down': w_down.astype(BF16),
    }

    tm_p = min(512, Tp)
    hp_, hs_ = x_prompt, x_sample
    outs_p = {k: [] for k in ('S', 'gb', 'sk', 'sv', 'sb', 'mk', 'mv', 'fb')}
    outs_s = {k: [] for k in ('S', 'gb', 'sk', 'sv', 'sb', 'fb')}
    for l in range(depth):
        lw = {k: v[l] for k, v in stacked.items()}
        mk, mv = _mem_kv(mem_prompt, lw['mem_in_norm_g'], lw['w_mk'], lw['w_mv'], lw['mem_k_norm_g'])
        hp_, S, gb, sb, fb, sk, sv = _layer(
            hp_, mk, mv,
            jnp.zeros((Bp, GDN_HEADS, GDN_DK, GDN_DV), F32),
            jnp.zeros((Bp, GDN_CONV - 1, GDN_CONV_CH), F32),
            jnp.zeros((Bp, SC_CONV - 1, SC_WIDTH), F32),
            jnp.zeros((Bp, FFN_CONV - 1, 2 * D_FF), F32),
            None, lw, tm_p, tm_p)
        outs_p['S'].append(S); outs_p['gb'].append(gb); outs_p['sb'].append(sb); outs_p['fb'].append(fb)
        outs_p['sk'].append(sk[:, Tp - keep:]); outs_p['sv'].append(sv[:, Tp - keep:])
        outs_p['mk'].append(mk.reshape(Bp, MEM_TOKENS, MEM_HEADS, MEM_HD))
        outs_p['mv'].append(mv.reshape(Bp, MEM_TOKENS, MEM_HEADS, MEM_HD))
        past = cache_swa_k.shape[2]
        hs_, S, gb, sb, fb, sk, sv = _layer(
            hs_, cache_mem_k[l].reshape(Bs, MEM_TOKENS, D_MODEL),
            cache_mem_v[l].reshape(Bs, MEM_TOKENS, D_MODEL),
            state_gdn[l], state_gdn_conv[l], state_sc_conv[l], state_ffn_conv[l],
            (cache_swa_k[l].reshape(Bs, past, SWA_KV), cache_swa_v[l].reshape(Bs, past, SWA_KV)),
            lw, Bs * Ts, Ts)
        outs_s['S'].append(S); outs_s['gb'].append(gb); outs_s['sb'].append(sb); outs_s['fb'].append(fb)
        outs_s['sk'].append(sk); outs_s['sv'].append(sv)
    st = jnp.stack
    return (hp_, hs_,
            st(outs_p['S']), st(outs_p['gb']), st(outs_p['sk']), st(outs_p['sv']), st(outs_p['sb']),
            st(outs_p['mk']), st(outs_p['mv']), st(outs_p['fb']),
            st(outs_s['S']), st(outs_s['gb']), st(outs_s['sk']), st(outs_s['sv']), st(outs_s['sb']),
            st(outs_s['fb']))
```

```python
import functools

import jax
import jax.numpy as jnp
from jax import lax
from jax.experimental import pallas as pl
from jax.experimental.pallas import tpu as pltpu

F32 = jnp.float32
BF16 = jnp.bfloat16

D_MODEL = 1024
CHUNK = 64
EPS = 1e-6
NEG_INF = -1e30
GDN_HEADS = 4
GDN_DK = 128
GDN_DV = 128
GDN_CONV = 4
GDN_QK = GDN_HEADS * GDN_DK
GDN_V = GDN_HEADS * GDN_DV
GDN_CONV_CH = 2 * GDN_QK + GDN_V
SWA_HEADS = 4
SWA_KV_HEADS = 2
SWA_GROUP = SWA_HEADS // SWA_KV_HEADS
SWA_HD = 64
WINDOW = 128
WIN_CHUNKS = WINDOW // CHUNK
SWA_Q = SWA_HEADS * SWA_HD
SWA_KV = SWA_KV_HEADS * SWA_HD
SWA_W = SWA_Q + 2 * SWA_KV
SC_WIDTH = 256
SC_CONV = 3
MEM_TOKENS = 256
MEM_HEADS = 4
MEM_HD = D_MODEL // MEM_HEADS
D_FF = 2816
FFN_CONV = 3
LANES = 128
SUBLANES = 8
AB_W = LANES
SC_W = 3 * SC_WIDTH
IN_COLS = (
    (0, GDN_CONV_CH),
    (GDN_CONV_CH, GDN_CONV_CH + GDN_V),
    (GDN_CONV_CH + GDN_V, GDN_CONV_CH + GDN_V + SWA_W),
    (GDN_CONV_CH + GDN_V + SWA_W, GDN_CONV_CH + GDN_V + SWA_W + SC_W),
    (GDN_CONV_CH + GDN_V + SWA_W + SC_W, GDN_CONV_CH + GDN_V + SWA_W + SC_W + AB_W),
)
D_IN_PAD = IN_COLS[-1][1]
FFN_COL_CHUNK = 256
VMEM_LIMIT = 56 * 1024 * 1024


def _dot(a, b):
    return jnp.dot(a.astype(BF16), b.astype(BF16), preferred_element_type=F32)


def _dot_nt(a, b):
    return lax.dot_general(a.astype(BF16), b.astype(BF16), (((1,), (1,)), ((), ())),
                           preferred_element_type=F32)


def _dot_tn(a, b):
    return lax.dot_general(a.astype(BF16), b.astype(BF16), (((0,), (0,)), ((), ())),
                           preferred_element_type=F32)


def _rms(x, g):
    return x * lax.rsqrt(jnp.mean(x * x, axis=-1, keepdims=True) + EPS) * g


def _silu(x):
    return x * jax.nn.sigmoid(x)


def _const_spec(shape):
    nd = len(shape)
    return pl.BlockSpec(shape, lambda *_: (0,) * nd)


def _params(sem):
    return pltpu.CompilerParams(dimension_semantics=sem, vmem_limit_bytes=VMEM_LIMIT)


def _in_proj_kernel(x_ref, g_ref, w_ref, qkv_ref, z_ref, swa_ref, sc_ref, ab_ref):
    xn = _rms(x_ref[...], g_ref[...]).astype(BF16)
    for o_ref, (lo, hi) in zip((qkv_ref, z_ref, swa_ref, sc_ref, ab_ref), IN_COLS):
        o_ref[...] = jnp.dot(xn, w_ref[:, lo:hi], preferred_element_type=F32)


def _in_proj(x2, g, w, tm):
    n = x2.shape[0]
    widths = [hi - lo for lo, hi in IN_COLS]
    return pl.pallas_call(
        _in_proj_kernel,
        grid=(n // tm,),
        in_specs=[pl.BlockSpec((tm, D_MODEL), lambda i: (i, 0)),
                  _const_spec((1, D_MODEL)),
                  _const_spec((D_MODEL, D_IN_PAD))],
        out_specs=[pl.BlockSpec((tm, wd), lambda i: (i, 0)) for wd in widths],
        out_shape=[jax.ShapeDtypeStruct((n, wd), F32) for wd in widths],
        compiler_params=_params(("arbitrary",)),
        name="in_proj",
    )(x2, g, w)


def _cumsum_rows(x):
    rows = x.shape[0]
    row = lax.broadcasted_iota(jnp.int32, x.shape, 0)
    s = 1
    while s < rows:
        x = x + jnp.where(row >= s, pltpu.roll(x, s, axis=0), 0.0)
        s *= 2
    return x


def _gdn_kernel(qkv_ref, ab_ref, z_ref, sc_ref, wconv_ref, wsc_ref, hp_ref, gn_ref,
                s0_ref, gbuf_ref, scbuf_ref,
                oa_ref, oc_ref, sout_ref, scout_ref,
                s_scr, xp_scr, scp_scr, *, bb, C):
    n = pl.program_id(1)
    pad = SUBLANES

    @pl.when(n == 0)
    def _init():
        s_scr[...] = s0_ref[...]
        xp_scr[:, pad - (GDN_CONV - 1):pad, :] = gbuf_ref[...]
        scp_scr[:, pad - (SC_CONV - 1):pad, :] = scbuf_ref[...]

    row_i = lax.broadcasted_iota(jnp.int32, (C, C), 0)
    col_j = lax.broadcasted_iota(jnp.int32, (C, C), 1)
    incl = row_i >= col_j
    strict = row_i > col_j
    neg_a = -jnp.exp(hp_ref[0:1, :])
    dt_b = hp_ref[1:2, :]
    n_sq = C.bit_length() - 2

    ys, gcs, gcts, betas = [], [], [], []
    for b in range(bb):
        x = qkv_ref[b]
        xp_scr[b, pad:pad + C, :] = x
        y = xp_scr[b, pad - 3:pad - 3 + C, :] * wconv_ref[0:1, :]
        y = y + xp_scr[b, pad - 2:pad - 2 + C, :] * wconv_ref[1:2, :]
        y = y + xp_scr[b, pad - 1:pad - 1 + C, :] * wconv_ref[2:3, :]
        y = y + x * wconv_ref[3:4, :]
        xp_scr[b, pad - 3:pad, :] = x[C - 3:C, :]
        ys.append(_silu(y))

        ab = ab_ref[b]
        t = ab + dt_b
        softplus = jnp.maximum(t, 0.0) + jnp.log1p(jnp.exp(-jnp.abs(t)))
        gc = _cumsum_rows(neg_a * softplus)
        gcs.append(gc)
        betas.append(jax.nn.sigmoid(ab))
        gcts.append(jnp.transpose(jnp.concatenate(
            [gc, jnp.zeros((LANES - C, LANES), F32)], axis=0)))

        scv = sc_ref[b]
        prod = scv[:, SC_WIDTH:2 * SC_WIDTH] * scv[:, 2 * SC_WIDTH:3 * SC_WIDTH]
        scp_scr[b, pad:pad + C, :] = prod
        cu = scp_scr[b, pad - 2:pad - 2 + C, :] * wsc_ref[0:1, :]
        cu = cu + scp_scr[b, pad - 1:pad - 1 + C, :] * wsc_ref[1:2, :]
        cu = cu + prod * wsc_ref[2:3, :]
        scp_scr[b, pad - 2:pad, :] = prod[C - 2:C, :]
        oc_ref[b] = scv[:, 0:SC_WIDTH] * cu

    pairs = [(b, h) for b in range(bb) for h in range(GDN_HEADS)]
    every = range(len(pairs))
    qs, ks, vs, kbs, bts, egs, e_incls, e_stricts, kds, gls = ([] for _ in range(10))
    for b, h in pairs:
        y, gc = ys[b], gcs[b]
        q = y[:, h * GDN_DK:(h + 1) * GDN_DK]
        k = y[:, GDN_QK + h * GDN_DK:GDN_QK + (h + 1) * GDN_DK]
        q = q * lax.rsqrt(jnp.sum(q * q, axis=-1, keepdims=True) + EPS) * (GDN_DK ** -0.5)
        k = k * lax.rsqrt(jnp.sum(k * k, axis=-1, keepdims=True) + EPS)
        gci = gc[:, h:h + 1]
        gcj = gcts[b][h:h + 1, 0:C]
        g_last = gc[C - 1:C, h:h + 1]
        e_incl = jnp.exp(jnp.where(incl, gci - gcj, NEG_INF))
        qs.append(q)
        ks.append(k)
        kbs.append(k.astype(BF16))
        vs.append(y[:, 2 * GDN_QK + h * GDN_DV:2 * GDN_QK + (h + 1) * GDN_DV])
        bts.append(betas[b][:, GDN_HEADS + h:GDN_HEADS + h + 1])
        egs.append(jnp.exp(gci))
        e_incls.append(e_incl)
        e_stricts.append(jnp.where(strict, e_incl, 0.0))
        kds.append(k * jnp.exp(g_last - gci))
        gls.append(jnp.exp(g_last))

    kks = [_dot_nt(kbs[i], kbs[i]) for i in every]
    qks = [_dot_nt(qs[i], kbs[i]) * e_incls[i] for i in every]
    s_prevs = [s_scr[b, h] for b, h in pairs]
    sbs = [s.astype(BF16) for s in s_prevs]
    o_inter = [_dot(qs[i] * egs[i], sbs[i]) for i in every]
    p_pows = [-(bts[i] * kks[i] * e_stricts[i]) for i in every]
    n_accs = list(p_pows)
    for _ in range(n_sq):
        pbs = [p.astype(BF16) for p in p_pows]
        p_pows = [_dot(pb, pb) for pb in pbs]
        n_accs = [n_accs[i] + p_pows[i] + _dot(p_pows[i], n_accs[i]) for i in every]
    rhss = [jnp.concatenate([bts[i] * vs[i], (bts[i] * egs[i]) * ks[i]], axis=-1) for i in every]
    sols = [rhss[i] + _dot(n_accs[i], rhss[i]) for i in every]
    us = [sols[i][:, :GDN_DV] - _dot(sols[i][:, GDN_DV:], sbs[i]) for i in every]
    outs = [o_inter[i] + _dot(qks[i], us[i]) for i in every]
    s_news = [gls[i] * s_prevs[i] + _dot_tn(kds[i], us[i]) for i in every]
    for i, (b, h) in enumerate(pairs):
        s_scr[b, h] = s_news[i]
        zh = z_ref[b, :, h * GDN_DV:(h + 1) * GDN_DV]
        oa_ref[b, :, h * GDN_DV:(h + 1) * GDN_DV] = _rms(outs[i], gn_ref[...]) * _silu(zh)

    @pl.when(n == pl.num_programs(1) - 1)
    def _fin():
        sout_ref[...] = s_scr[...]
        scout_ref[...] = scp_scr[:, pad - (SC_CONV - 1):pad, :]


def _gdn(qkv, ab, z, sc, wconv, wsc, hp, gn, s0, gbuf, scbuf, C, bb):
    B, T, _ = qkv.shape
    N = T // C
    tok = lambda w: pl.BlockSpec((bb, C, w), lambda i, n: (i, n, 0))
    per_b = lambda *s: pl.BlockSpec((bb,) + s, lambda i, n: (i,) + (0,) * len(s))
    return pl.pallas_call(
        functools.partial(_gdn_kernel, bb=bb, C=C),
        grid=(B // bb, N),
        in_specs=[tok(GDN_CONV_CH), tok(AB_W), tok(GDN_V), tok(SC_W),
                  _const_spec((GDN_CONV, GDN_CONV_CH)), _const_spec((SC_CONV, SC_WIDTH)),
                  _const_spec((SUBLANES, LANES)), _const_spec((1, GDN_DV)),
                  per_b(GDN_HEADS, GDN_DK, GDN_DV), per_b(GDN_CONV - 1, GDN_CONV_CH),
                  per_b(SC_CONV - 1, SC_WIDTH)],
        out_specs=[tok(GDN_V), tok(SC_WIDTH),
                   per_b(GDN_HEADS, GDN_DK, GDN_DV), per_b(SC_CONV - 1, SC_WIDTH)],
        out_shape=[jax.ShapeDtypeStruct((B, T, GDN_V), F32),
                   jax.ShapeDtypeStruct((B, T, SC_WIDTH), F32),
                   jax.ShapeDtypeStruct((B, GDN_HEADS, GDN_DK, GDN_DV), F32),
                   jax.ShapeDtypeStruct((B, SC_CONV - 1, SC_WIDTH), F32)],
        scratch_shapes=[pltpu.VMEM((bb, GDN_HEADS, GDN_DK, GDN_DV), F32),
                        pltpu.VMEM((bb, SUBLANES + C, GDN_CONV_CH), F32),
                        pltpu.VMEM((bb, SUBLANES + C, SC_WIDTH), F32)],
        compiler_params=_params(("arbitrary", "arbitrary")),
        name="gdn",
    )(qkv, ab, z, sc, wconv, wsc, hp, gn, s0, gbuf, scbuf)


def _swa_norm_k(x, gk):
    return jnp.concatenate(
        [_rms(x[:, SWA_Q + i * SWA_HD:SWA_Q + (i + 1) * SWA_HD], gk) for i in range(SWA_KV_HEADS)],
        axis=-1)


def _swa_norm_q(x, gq):
    return [_rms(x[:, h * SWA_HD:(h + 1) * SWA_HD], gq) for h in range(SWA_HEADS)]


def _group_queries(qn, g, r0, r1, sinks):
    heads = range(g * SWA_GROUP, (g + 1) * SWA_GROUP)
    q = jnp.concatenate([qn[h][r0:r1] for h in heads], axis=0)
    sink = jnp.concatenate(
        [jnp.broadcast_to(sinks[:, h:h + 1], (r1 - r0, 1)) for h in heads], axis=0)
    return q, sink


def _attend_many(problems):
    scores = [_dot_nt(q, k) * (SWA_HD ** -0.5) for q, k, _, _, _ in problems]
    probs = []
    for s, (_, _, _, valid, sink) in zip(scores, problems):
        if valid is not None:
            s = jnp.where(valid, s, NEG_INF)
        m = jnp.maximum(jnp.max(s, axis=-1, keepdims=True), sink)
        p = jnp.exp(s - m)
        probs.append(p / (jnp.sum(p, axis=-1, keepdims=True) + jnp.exp(sink - m)))
    return [_dot(p, pr[2]) for p, pr in zip(probs, problems)]


def _ungroup(outs, rows):
    return jnp.concatenate(
        [o[j * rows:(j + 1) * rows] for o in outs for j in range(SWA_GROUP)], axis=-1)


def _swa_prompt_kernel(x_ref, halo_ref, gq_ref, gk_ref, sink_ref, o_ref, kn_ref, *, qb):
    i = pl.program_id(1)
    band = (WIN_CHUNKS + 1) * CHUNK
    n_chunks = qb // CHUNK
    x = x_ref[...]
    hx = halo_ref[...]
    kn = _swa_norm_k(x, gk_ref[...])
    kn_ref[...] = kn
    k_all = jnp.concatenate([_swa_norm_k(hx, gk_ref[...]), kn], axis=0)
    v_all = jnp.concatenate([hx[:, SWA_Q + SWA_KV:SWA_W], x[:, SWA_Q + SWA_KV:SWA_W]], axis=0)
    qn = _swa_norm_q(x, gq_ref[...])
    local_chunk = lax.broadcasted_iota(jnp.int32, (1, band), 1) // CHUNK
    problems = []
    for c in range(n_chunks):
        valid = local_chunk + (i * n_chunks + c - WIN_CHUNKS) >= 0
        for g in range(SWA_KV_HEADS):
            q, sink = _group_queries(qn, g, c * CHUNK, (c + 1) * CHUNK, sink_ref[...])
            cols = slice(g * SWA_HD, (g + 1) * SWA_HD)
            problems.append((q, k_all[c * CHUNK:c * CHUNK + band, cols],
                             v_all[c * CHUNK:c * CHUNK + band, cols], valid, sink))
    outs = _attend_many(problems)
    for c in range(n_chunks):
        o_ref[c * CHUNK:(c + 1) * CHUNK, :] = _ungroup(
            outs[c * SWA_KV_HEADS:(c + 1) * SWA_KV_HEADS], CHUNK)


def _swa_prompt(swa, gq, gk, sinks):
    B, T, _ = swa.shape
    qb = min(256, T)
    assert T % qb == 0 and qb % WINDOW == 0
    halo_per_block = qb // WINDOW
    return pl.pallas_call(
        functools.partial(_swa_prompt_kernel, qb=qb),
        grid=(B, T // qb),
        in_specs=[pl.BlockSpec((None, qb, SWA_W), lambda b, i: (b, i, 0)),
                  pl.BlockSpec((None, WINDOW, SWA_W),
                               lambda b, i: (b, jnp.maximum(i * halo_per_block - 1, 0), 0)),
                  _const_spec((1, SWA_HD)), _const_spec((1, SWA_HD)), _const_spec((1, LANES))],
        out_specs=[pl.BlockSpec((None, qb, SWA_Q), lambda b, i: (b, i, 0)),
                   pl.BlockSpec((None, qb, SWA_KV), lambda b, i: (b, i, 0))],
        out_shape=[jax.ShapeDtypeStruct((B, T, SWA_Q), F32),
                   jax.ShapeDtypeStruct((B, T, SWA_KV), F32)],
        compiler_params=_params(("arbitrary", "arbitrary")),
        name="swa_prompt",
    )(swa, swa, gq, gk, sinks)


def _swa_sample_kernel(x_ref, kp_ref, vp_ref, gq_ref, gk_ref, sink_ref, o_ref, kn_ref, *, bb, T):
    problems = []
    for b in range(bb):
        x = x_ref[b]
        kn = _swa_norm_k(x, gk_ref[...])
        kn_ref[b] = kn
        k_all = jnp.concatenate([kp_ref[b], kn], axis=0)
        v_all = jnp.concatenate([vp_ref[b], x[:, SWA_Q + SWA_KV:SWA_W]], axis=0)
        qn = _swa_norm_q(x, gq_ref[...])
        for g in range(SWA_KV_HEADS):
            q, sink = _group_queries(qn, g, 0, T, sink_ref[...])
            cols = slice(g * SWA_HD, (g + 1) * SWA_HD)
            problems.append((q, k_all[:, cols], v_all[:, cols], None, sink))
    outs = _attend_many(problems)
    for b in range(bb):
        o_ref[b] = _ungroup(outs[b * SWA_KV_HEADS:(b + 1) * SWA_KV_HEADS], T)


def _swa_sample(swa, k_past, v_past, gq, gk, sinks):
    B, T, _ = swa.shape
    past = k_past.shape[1]
    bb = 4 if B % 4 == 0 else 1
    blk = lambda r, w: pl.BlockSpec((bb, r, w), lambda b: (b, 0, 0))
    return pl.pallas_call(
        functools.partial(_swa_sample_kernel, bb=bb, T=T),
        grid=(B // bb,),
        in_specs=[blk(T, SWA_W), blk(past, SWA_KV), blk(past, SWA_KV),
                  _const_spec((1, SWA_HD)), _const_spec((1, SWA_HD)), _const_spec((1, LANES))],
        out_specs=[blk(T, SWA_Q), blk(T, SWA_KV)],
        out_shape=[jax.ShapeDtypeStruct((B, T, SWA_Q), F32),
                   jax.ShapeDtypeStruct((B, T, SWA_KV), F32)],
        compiler_params=_params(("arbitrary",)),
        name="swa_sample",
    )(swa, k_past, v_past, gq, gk, sinks)


def _mem_kv_kernel(m_ref, g_ref, wk_ref, wv_ref, gk_ref, k_ref, v_ref):
    m = _rms(m_ref[...], g_ref[...]).astype(BF16)
    k = jnp.dot(m, wk_ref[...], preferred_element_type=F32)
    k_ref[...] = jnp.concatenate(
        [_rms(k[:, h * MEM_HD:(h + 1) * MEM_HD], gk_ref[...]) for h in range(MEM_HEADS)], axis=-1)
    v_ref[...] = jnp.dot(m, wv_ref[...], preferred_element_type=F32)


def _mem_kv(mem, g, wk, wv, gk):
    B, M, _ = mem.shape
    blk = pl.BlockSpec((None, M, D_MODEL), lambda b: (b, 0, 0))
    return pl.pallas_call(
        _mem_kv_kernel,
        grid=(B,),
        in_specs=[blk, _const_spec((1, D_MODEL)), _const_spec((D_MODEL, D_MODEL)),
                  _const_spec((D_MODEL, D_MODEL)), _const_spec((1, MEM_HD))],
        out_specs=[blk, blk],
        out_shape=[jax.ShapeDtypeStruct((B, M, D_MODEL), F32)] * 2,
        compiler_params=_params(("arbitrary",)),
        name="mem_kv",
    )(mem, g, wk, wv, gk)


def _post_kernel(x_ref, a_ref, b_ref, c_ref, wo_ref, gm_ref, wq_ref, gq_ref, mk_ref, mv_ref,
                 wmo_ref, o_ref):
    h = x_ref[...]
    h = h + (jnp.dot(a_ref[...].astype(BF16), wo_ref[0:GDN_V, :], preferred_element_type=F32)
             + jnp.dot(b_ref[...].astype(BF16), wo_ref[GDN_V:GDN_V + SWA_Q, :],
                       preferred_element_type=F32)
             + jnp.dot(c_ref[...].astype(BF16), wo_ref[GDN_V + SWA_Q:, :],
                       preferred_element_type=F32))
    hn = _rms(h, gm_ref[...]).astype(BF16)
    q = jnp.dot(hn, wq_ref[...], preferred_element_type=F32)
    outs = []
    for hd in range(MEM_HEADS):
        sl = slice(hd * MEM_HD, (hd + 1) * MEM_HD)
        qh = _rms(q[:, sl], gq_ref[...])
        s = _dot_nt(qh, mk_ref[:, sl]) * (MEM_HD ** -0.5)
        m = jnp.max(s, axis=-1, keepdims=True)
        p = jnp.exp(s - m)
        p = p / jnp.sum(p, axis=-1, keepdims=True)
        outs.append(_dot(p, mv_ref[:, sl]))
    o = jnp.concatenate(outs, axis=-1).astype(BF16)
    o_ref[...] = h + jnp.dot(o, wmo_ref[...], preferred_element_type=F32)


def _post(x, oa, ob, oc, wo, gm, wq, gq, mk, mv, wmo, tm):
    B, T, _ = x.shape
    tok = lambda w: pl.BlockSpec((None, tm, w), lambda b, i: (b, i, 0))
    mem = pl.BlockSpec((None, MEM_TOKENS, D_MODEL), lambda b, i: (b, 0, 0))
    sq = _const_spec((D_MODEL, D_MODEL))
    return pl.pallas_call(
        _post_kernel,
        grid=(B, T // tm),
        in_specs=[tok(D_MODEL), tok(GDN_V), tok(SWA_Q), tok(SC_WIDTH), sq,
                  _const_spec((1, D_MODEL)), sq, _const_spec((1, MEM_HD)), mem, mem, sq],
        out_specs=tok(D_MODEL),
        out_shape=jax.ShapeDtypeStruct((B, T, D_MODEL), F32),
        compiler_params=_params(("arbitrary", "arbitrary")),
        name="post",
    )(x, oa, ob, oc, wo, gm, wq, gq, mk, mv, wmo)


def _ffn_kernel(h_ref, g_ref, wup_ref, wc_ref, wdn_ref, buf_ref, o_ref, st_ref,
                carry_scr, u_scr, *, tm):
    i = pl.program_id(1)
    pad = SUBLANES
    cw = FFN_COL_CHUNK

    @pl.when(i == 0)
    def _init():
        carry_scr[pad - (FFN_CONV - 1):pad, :] = buf_ref[...]

    h = h_ref[...]
    xn = _rms(h, g_ref[...]).astype(BF16)
    acc = h
    for j in range(D_FF // cw):
        conv = []
        for base in (0, D_FF):
            lo = base + j * cw
            u = jnp.dot(xn, wup_ref[:, lo:lo + cw], preferred_element_type=F32)
            u_scr[pad - 2:pad, :] = carry_scr[pad - 2:pad, lo:lo + cw]
            u_scr[pad:pad + tm, :] = u
            y = u_scr[pad - 2:pad - 2 + tm, :] * wc_ref[0:1, lo:lo + cw]
            y = y + u_scr[pad - 1:pad - 1 + tm, :] * wc_ref[1:2, lo:lo + cw]
            y = y + u * wc_ref[2:3, lo:lo + cw]
            carry_scr[pad - 2:pad, lo:lo + cw] = u[tm - 2:tm, :]
            conv.append(y)
        act = (_silu(conv[0]) * conv[1]).astype(BF16)
        acc = acc + jnp.dot(act, wdn_ref[j * cw:(j + 1) * cw, :], preferred_element_type=F32)
    o_ref[...] = acc

    @pl.when(i == pl.num_programs(1) - 1)
    def _fin():
        st_ref[...] = carry_scr[pad - (FFN_CONV - 1):pad, :]


def _ffn(h, g, wup, wc, wdn, buf, tm):
    B, T, _ = h.shape
    tok = pl.BlockSpec((None, tm, D_MODEL), lambda b, i: (b, i, 0))
    st = pl.BlockSpec((None, FFN_CONV - 1, 2 * D_FF), lambda b, i: (b, 0, 0))
    return pl.pallas_call(
        functools.partial(_ffn_kernel, tm=tm),
        grid=(B, T // tm),
        in_specs=[tok, _const_spec((1, D_MODEL)), _const_spec((D_MODEL, 2 * D_FF)),
                  _const_spec((FFN_CONV, 2 * D_FF)), _const_spec((D_FF, D_MODEL)), st],
        out_specs=[tok, st],
        out_shape=[jax.ShapeDtypeStruct((B, T, D_MODEL), F32),
                   jax.ShapeDtypeStruct((B, FFN_CONV - 1, 2 * D_FF), F32)],
        scratch_shapes=[pltpu.VMEM((SUBLANES, 2 * D_FF), F32),
                        pltpu.VMEM((SUBLANES + tm, FFN_COL_CHUNK), F32)],
        compiler_params=_params(("arbitrary", "arbitrary")),
        name="ffn",
    )(h, g, wup, wc, wdn, buf)


def _layer(x, mem_k, mem_v, s0, gbuf, scbuf, fbuf, swa_past, lw, tm_tok, tm_seq):
    B, T, _ = x.shape
    C = CHUNK if T % CHUNK == 0 else T
    qkv, z, swa, sc, ab = _in_proj(x.reshape(B * T, D_MODEL), lw['norm_mix_g'], lw['w_in'], tm_tok)
    r3 = lambda a: a.reshape(B, T, a.shape[-1])
    qkv, z, swa, sc, ab = r3(qkv), r3(z), r3(swa), r3(sc), r3(ab)
    oa, oc, s_new, sc_new = _gdn(qkv, ab, z, sc, lw['w_gdn_conv'], lw['w_sc_conv'], lw['hp'],
                                 lw['gdn_norm_g'], s0, gbuf, scbuf, C, 2)
    if swa_past is None:
        ob, kn = _swa_prompt(swa, lw['swa_q_norm_g'], lw['swa_k_norm_g'], lw['swa_sinks'])
    else:
        ob, kn = _swa_sample(swa, swa_past[0], swa_past[1], lw['swa_q_norm_g'], lw['swa_k_norm_g'],
                             lw['swa_sinks'])
    h = _post(x, oa, ob, oc, lw['w_o'], lw['norm_mem_g'], lw['w_mq'], lw['mem_q_norm_g'],
              mem_k, mem_v, lw['w_mo'], tm_seq)
    h, f_new = _ffn(h, lw['norm_ffn_g'], lw['w_up'], lw['w_ffn_conv'], lw['w_down'], fbuf, tm_seq)
    gb_new = qkv[:, T - (GDN_CONV - 1):, :]
    sk = kn.reshape(B, T, SWA_KV_HEADS, SWA_HD)
    sv = swa[:, :, SWA_Q + SWA_KV:].reshape(B, T, SWA_KV_HEADS, SWA_HD)
    return h, s_new, gb_new, sc_new, f_new, sk, sv


def kernel(x_prompt, x_sample, mem_prompt, state_gdn, state_gdn_conv, cache_swa_k, cache_swa_v, state_sc_conv, cache_mem_k, cache_mem_v, state_ffn_conv, norm_mix_g, w_in, w_gdn_conv, gdn_a_log, gdn_dt_bias, gdn_norm_g, swa_q_norm_g, swa_k_norm_g, swa_sinks, w_sc_conv, w_o, norm_mem_g, mem_in_norm_g, w_mq, w_mk, w_mv, mem_q_norm_g, mem_k_norm_g, w_mo, norm_ffn_g, w_up, w_ffn_conv, w_down):
    depth = w_in.shape[0]
    Bp, Tp, _ = x_prompt.shape
    Bs, Ts, _ = x_sample.shape
    keep = min(WINDOW, Tp)

    qz = GDN_CONV_CH + GDN_V
    w_in_r = jnp.concatenate(
        [w_in[:, :, :qz], w_in[:, :, qz + 2 * GDN_HEADS:], w_in[:, :, qz:qz + 2 * GDN_HEADS],
         jnp.zeros((depth, D_MODEL, AB_W - 2 * GDN_HEADS), w_in.dtype)], axis=-1).astype(BF16)
    hp = jnp.zeros((depth, SUBLANES, LANES), F32)
    hp = hp.at[:, 0, :GDN_HEADS].set(gdn_a_log).at[:, 1, :GDN_HEADS].set(gdn_dt_bias)
    sinks = jnp.zeros((depth, 1, LANES), F32).at[:, 0, :SWA_HEADS].set(swa_sinks)
    row = lambda a: a[:, None, :]
    stacked = {
        'norm_mix_g': row(norm_mix_g), 'w_in': w_in_r, 'w_gdn_conv': w_gdn_conv, 'hp': hp,
        'gdn_norm_g': row(gdn_norm_g), 'swa_q_norm_g': row(swa_q_norm_g),
        'swa_k_norm_g': row(swa_k_norm_g), 'swa_sinks': sinks, 'w_sc_conv': w_sc_conv,
        'w_o': w_o.astype(BF16), 'norm_mem_g': row(norm_mem_g), 'mem_in_norm_g': row(mem_in_norm_g),
        'w_mq': w_mq.astype(BF16), 'w_mk': w_mk.astype(BF16), 'w_mv': w_mv.astype(BF16),
        'mem_q_norm_g': row(mem_q_norm_g), 'mem_k_norm_g': row(mem_k_norm_g),
        'w_mo': w_mo.astype(BF16), 'norm_ffn_g': row(norm_ffn_g), 'w_up': w_up.astype(BF16),
        'w_ffn_conv': w_ffn_conv, 'w_down': w_down.astype(BF16),
    }

    tm_p = min(512, Tp)
    hp_, hs_ = x_prompt, x_sample
    outs_p = {k: [] for k in ('S', 'gb', 'sk', 'sv', 'sb', 'mk', 'mv', 'fb')}
    outs_s = {k: [] for k in ('S', 'gb', 'sk', 'sv', 'sb', 'fb')}
    for l in range(depth):
        lw = {k: v[l] for k, v in stacked.items()}
        mk, mv = _mem_kv(mem_prompt, lw['mem_in_norm_g'], lw['w_mk'], lw['w_mv'], lw['mem_k_norm_g'])
        hp_, S, gb, sb, fb, sk, sv = _layer(
            hp_, mk, mv,
            jnp.zeros((Bp, GDN_HEADS, GDN_DK, GDN_DV), F32),
            jnp.zeros((Bp, GDN_CONV - 1, GDN_CONV_CH), F32),
            jnp.zeros((Bp, SC_CONV - 1, SC_WIDTH), F32),
            jnp.zeros((Bp, FFN_CONV - 1, 2 * D_FF), F32),
            None, lw, tm_p, tm_p)
        outs_p['S'].append(S); outs_p['gb'].append(gb); outs_p['sb'].append(sb); outs_p['fb'].append(fb)
        outs_p['sk'].append(sk[:, Tp - keep:]); outs_p['sv'].append(sv[:, Tp - keep:])
        outs_p['mk'].append(mk.reshape(Bp, MEM_TOKENS, MEM_HEADS, MEM_HD))
        outs_p['mv'].append(mv.reshape(Bp, MEM_TOKENS, MEM_HEADS, MEM_HD))
        past = cache_swa_k.shape[2]
        hs_, S, gb, sb, fb, sk, sv = _layer(
            hs_, cache_mem_k[l].reshape(Bs, MEM_TOKENS, D_MODEL),
            cache_mem_v[l].reshape(Bs, MEM_TOKENS, D_MODEL),
            state_gdn[l], state_gdn_conv[l], state_sc_conv[l], state_ffn_conv[l],
            (cache_swa_k[l].reshape(Bs, past, SWA_KV), cache_swa_v[l].reshape(Bs, past, SWA_KV)),
            lw, Bs * Ts, Ts)
        outs_s['S'].append(S); outs_s['gb'].append(gb); outs_s['sb'].append(sb); outs_s['fb'].append(fb)
        outs_s['sk'].append(sk); outs_s['sv'].append(sv)
    st = jnp.stack
    return (hp_, hs_,
            st(outs_p['S']), st(outs_p['gb']), st(outs_p['sk']), st(outs_p['sv']), st(outs_p['sb']),
            st(outs_p['mk']), st(outs_p['mv']), st(outs_p['fb']),
            st(outs_s['S']), st(outs_s['gb']), st(outs_s['sk']), st(outs_s['sv']), st(outs_s['sb']),
            st(outs_s['fb']))
```

```python
import functools

import jax
import jax.numpy as jnp
from jax import lax
from jax.experimental import pallas as pl
from jax.experimental.pallas import tpu as pltpu

F32 = jnp.float32
BF16 = jnp.bfloat16

D_MODEL = 1024
CHUNK = 64
EPS = 1e-6
NEG_INF = -1e30
GDN_HEADS = 4
GDN_DK = 128
GDN_DV = 128
GDN_CONV = 4
GDN_QK = GDN_HEADS * GDN_DK
GDN_V = GDN_HEADS * GDN_DV
GDN_CONV_CH = 2 * GDN_QK + GDN_V
SWA_HEADS = 4
SWA_KV_HEADS = 2
SWA_GROUP = SWA_HEADS // SWA_KV_HEADS
SWA_HD = 64
WINDOW = 128
WIN_CHUNKS = WINDOW // CHUNK
SWA_Q = SWA_HEADS * SWA_HD
SWA_KV = SWA_KV_HEADS * SWA_HD
SWA_W = SWA_Q + 2 * SWA_KV
SC_WIDTH = 256
SC_CONV = 3
MEM_TOKENS = 256
MEM_HEADS = 4
MEM_HD = D_MODEL // MEM_HEADS
D_FF = 2816
FFN_CONV = 3
LANES = 128
SUBLANES = 8
N_SLABS = D_MODEL // LANES
AB_W = LANES
SC_W = 3 * SC_WIDTH
IN_COLS = (
    (0, GDN_CONV_CH),
    (GDN_CONV_CH, GDN_CONV_CH + GDN_V),
    (GDN_CONV_CH + GDN_V, GDN_CONV_CH + GDN_V + SWA_W),
    (GDN_CONV_CH + GDN_V + SWA_W, GDN_CONV_CH + GDN_V + SWA_W + SC_W),
    (GDN_CONV_CH + GDN_V + SWA_W + SC_W, GDN_CONV_CH + GDN_V + SWA_W + SC_W + AB_W),
)
D_IN_PAD = IN_COLS[-1][1]
FFN_COL_CHUNKS = ((0, 1024), (1024, 2048), (2048, D_FF))
FFN_SUB_TILES = 2
VMEM_LIMIT = 56 * 1024 * 1024
TOKEN_TILE = 512
SWA_QUERY_BLOCK = 256
MAX_ROWS_SMALL_SEQ = 128


def _seq_tiles(B, T):
    if T >= TOKEN_TILE:
        return 1, TOKEN_TILE
    nb = max(1, min(B, MAX_ROWS_SMALL_SEQ // T))
    while B % nb:
        nb -= 1
    return nb, T


def _dot(a, b):
    return jnp.dot(a.astype(BF16), b.astype(BF16), preferred_element_type=F32)


def _dot_nt(a, b):
    return lax.dot_general(a.astype(BF16), b.astype(BF16), (((1,), (1,)), ((), ())),
                           preferred_element_type=F32)


def _dot_tn(a, b):
    return lax.dot_general(a.astype(BF16), b.astype(BF16), (((0,), (0,)), ((), ())),
                           preferred_element_type=F32)


def _rms(x, g):
    return x * lax.rsqrt(jnp.mean(x * x, axis=-1, keepdims=True) + EPS) * g


def _silu(x):
    return x * jax.nn.sigmoid(x)


def _lspec(l, shape, single_buffer=False):
    nd = len(shape)
    mode = dict(pipeline_mode=pl.Buffered(1)) if single_buffer else {}
    return pl.BlockSpec((None,) + tuple(shape), lambda *_: (l,) + (0,) * nd, **mode)


def _params(sem):
    return pltpu.CompilerParams(dimension_semantics=sem, vmem_limit_bytes=VMEM_LIMIT)


def _to_slabs(x):
    B, T, _ = x.shape
    return x.reshape(B, T, N_SLABS, LANES).transpose(0, 2, 1, 3)


def _from_slabs(xs):
    B, _, T, _ = xs.shape
    return xs.transpose(0, 2, 1, 3).reshape(B, T, D_MODEL)


def _slab_spec(nb, tm):
    return pl.BlockSpec((nb, N_SLABS, tm, LANES), lambda b, i: (b, 0, i, 0))


def _rows_from_slabs(ref, nb):
    seqs = [jnp.concatenate([ref[b, c] for c in range(N_SLABS)], axis=-1) for b in range(nb)]
    return seqs[0] if nb == 1 else jnp.concatenate(seqs, axis=0)


def _rows_to_slabs(ref, val, nb, tm):
    for b in range(nb):
        for c in range(N_SLABS):
            ref[b, c] = val[b * tm:(b + 1) * tm, c * LANES:(c + 1) * LANES]


def _in_proj_kernel(x_ref, g_ref, w_ref, qkv_ref, z_ref, swa_ref, sc_ref, ab_ref, *, nb, tm):
    xn = _rms(_rows_from_slabs(x_ref, nb), g_ref[...]).astype(BF16)
    for o_ref, (lo, hi) in zip((qkv_ref, z_ref, swa_ref, sc_ref, ab_ref), IN_COLS):
        res = jnp.dot(xn, w_ref[:, lo:hi], preferred_element_type=F32)
        for b in range(nb):
            o_ref[b] = res[b * tm:(b + 1) * tm]


def _in_proj(xs, l, p):
    B, _, T, _ = xs.shape
    nb, tm = _seq_tiles(B, T)
    widths = [hi - lo for lo, hi in IN_COLS]
    return pl.pallas_call(
        functools.partial(_in_proj_kernel, nb=nb, tm=tm),
        grid=(B // nb, T // tm),
        in_specs=[_slab_spec(nb, tm), _lspec(l, (1, D_MODEL)), _lspec(l, (D_MODEL, D_IN_PAD))],
        out_specs=[pl.BlockSpec((nb, tm, wd), lambda b, i: (b, i, 0)) for wd in widths],
        out_shape=[jax.ShapeDtypeStruct((B, T, wd), F32) for wd in widths],
        compiler_params=_params(("arbitrary", "arbitrary")),
        name="in_proj",
    )(xs, p['norm_mix_g'], p['w_in'])


def _cumsum_rows(x):
    rows = x.shape[0]
    row = lax.broadcasted_iota(jnp.int32, x.shape, 0)
    s = 1
    while s < rows:
        x = x + jnp.where(row >= s, pltpu.roll(x, s, axis=0), 0.0)
        s *= 2
    return x


def _gdn_kernel(*refs, bb, C, has_state):
    qkv_ref, ab_ref, z_ref, sc_ref, wconv_ref, wsc_ref, hp_ref, gn_ref = refs[:8]
    state_refs = refs[8:11] if has_state else ()
    oa_ref, oc_ref, sout_ref, scout_ref, s_scr, xp_scr, scp_scr = refs[8 + len(state_refs):]
    n = pl.program_id(1)
    pad = SUBLANES

    @pl.when(n == 0)
    def _init():
        if has_state:
            s0_ref, gbuf_ref, scbuf_ref = state_refs
            s_scr[...] = s0_ref[...]
            xp_scr[:, pad - (GDN_CONV - 1):pad, :] = gbuf_ref[...]
            scp_scr[:, pad - (SC_CONV - 1):pad, :] = scbuf_ref[...]
        else:
            s_scr[...] = jnp.zeros_like(s_scr)
            xp_scr[:, 0:pad, :] = jnp.zeros((bb, pad, GDN_CONV_CH), F32)
            scp_scr[:, 0:pad, :] = jnp.zeros((bb, pad, SC_WIDTH), F32)

    row_i = lax.broadcasted_iota(jnp.int32, (C, C), 0)
    col_j = lax.broadcasted_iota(jnp.int32, (C, C), 1)
    incl = row_i >= col_j
    strict = row_i > col_j
    neg_a = -jnp.exp(hp_ref[0:1, :])
    dt_b = hp_ref[1:2, :]
    n_sq = C.bit_length() - 2

    ys, gcs, gcts, betas = [], [], [], []
    for b in range(bb):
        x = qkv_ref[b]
        xp_scr[b, pad:pad + C, :] = x
        y = xp_scr[b, pad - 3:pad - 3 + C, :] * wconv_ref[0:1, :]
        y = y + xp_scr[b, pad - 2:pad - 2 + C, :] * wconv_ref[1:2, :]
        y = y + xp_scr[b, pad - 1:pad - 1 + C, :] * wconv_ref[2:3, :]
        y = y + x * wconv_ref[3:4, :]
        xp_scr[b, pad - 3:pad, :] = x[C - 3:C, :]
        ys.append(_silu(y))

        ab = ab_ref[b]
        t = ab + dt_b
        softplus = jnp.maximum(t, 0.0) + jnp.log1p(jnp.exp(-jnp.abs(t)))
        gc = _cumsum_rows(neg_a * softplus)
        gcs.append(gc)
        betas.append(jax.nn.sigmoid(ab))
        gcts.append(jnp.transpose(jnp.concatenate(
            [gc, jnp.zeros((LANES - C, LANES), F32)], axis=0)))

        scv = sc_ref[b]
        prod = scv[:, SC_WIDTH:2 * SC_WIDTH] * scv[:, 2 * SC_WIDTH:3 * SC_WIDTH]
        scp_scr[b, pad:pad + C, :] = prod
        cu = scp_scr[b, pad - 2:pad - 2 + C, :] * wsc_ref[0:1, :]
        cu = cu + scp_scr[b, pad - 1:pad - 1 + C, :] * wsc_ref[1:2, :]
        cu = cu + prod * wsc_ref[2:3, :]
        scp_scr[b, pad - 2:pad, :] = prod[C - 2:C, :]
        oc_ref[b] = scv[:, 0:SC_WIDTH] * cu

    pairs = [(b, h) for b in range(bb) for h in range(GDN_HEADS)]
    every = range(len(pairs))
    qs, ks, vs, kbs, bts, egs, e_incls, e_stricts, kds, gls = ([] for _ in range(10))
    for b, h in pairs:
        y, gc = ys[b], gcs[b]
        q = y[:, h * GDN_DK:(h + 1) * GDN_DK]
        k = y[:, GDN_QK + h * GDN_DK:GDN_QK + (h + 1) * GDN_DK]
        q = q * lax.rsqrt(jnp.sum(q * q, axis=-1, keepdims=True) + EPS) * (GDN_DK ** -0.5)
        k = k * lax.rsqrt(jnp.sum(k * k, axis=-1, keepdims=True) + EPS)
        gci = gc[:, h:h + 1]
        gcj = gcts[b][h:h + 1, 0:C]
        g_last = gc[C - 1:C, h:h + 1]
        e_incl = jnp.exp(jnp.where(incl, gci - gcj, NEG_INF))
        qs.append(q)
        ks.append(k)
        kbs.append(k.astype(BF16))
        vs.append(y[:, 2 * GDN_QK + h * GDN_DV:2 * GDN_QK + (h + 1) * GDN_DV])
        bts.append(betas[b][:, GDN_HEADS + h:GDN_HEADS + h + 1])
        egs.append(jnp.exp(gci))
        e_incls.append(e_incl)
        e_stricts.append(jnp.where(strict, e_incl, 0.0))
        kds.append(k * jnp.exp(g_last - gci))
        gls.append(jnp.exp(g_last))

    kks = [_dot_nt(kbs[i], kbs[i]) for i in every]
    qks = [_dot_nt(qs[i], kbs[i]) * e_incls[i] for i in every]
    s_prevs = [s_scr[b, h] for b, h in pairs]
    sbs = [s.astype(BF16) for s in s_prevs]
    o_inter = [_dot(qs[i] * egs[i], sbs[i]) for i in every]
    p_pows = [-(bts[i] * kks[i] * e_stricts[i]) for i in every]
    n_accs = list(p_pows)
    for _ in range(n_sq):
        pbs = [p.astype(BF16) for p in p_pows]
        p_pows = [_dot(pb, pb) for pb in pbs]
        n_accs = [n_accs[i] + p_pows[i] + _dot(p_pows[i], n_accs[i]) for i in every]
    rhss = [jnp.concatenate([bts[i] * vs[i], (bts[i] * egs[i]) * ks[i]], axis=-1) for i in every]
    sols = [rhss[i] + _dot(n_accs[i], rhss[i]) for i in every]
    us = [sols[i][:, :GDN_DV] - _dot(sols[i][:, GDN_DV:], sbs[i]) for i in every]
    outs = [o_inter[i] + _dot(qks[i], us[i]) for i in every]
    s_news = [gls[i] * s_prevs[i] + _dot_tn(kds[i], us[i]) for i in every]
    for i, (b, h) in enumerate(pairs):
        s_scr[b, h] = s_news[i]
        zh = z_ref[b, :, h * GDN_DV:(h + 1) * GDN_DV]
        oa_ref[b, :, h * GDN_DV:(h + 1) * GDN_DV] = _rms(outs[i], gn_ref[...]) * _silu(zh)

    @pl.when(n == pl.num_programs(1) - 1)
    def _fin():
        sout_ref[...] = s_scr[...]
        scout_ref[...] = scp_scr[:, pad - (SC_CONV - 1):pad, :]


def _gdn(qkv, ab, z, sc, l, p, states):
    B, T, _ = qkv.shape
    C = CHUNK if T % CHUNK == 0 else T
    bb = 2 if B % 2 == 0 else 1
    tok = lambda w: pl.BlockSpec((bb, C, w), lambda i, n: (i, n, 0))
    per_b = lambda *s: pl.BlockSpec((bb,) + s, lambda i, n: (i,) + (0,) * len(s))
    per_lb = lambda *s: pl.BlockSpec((None, bb) + s, lambda i, n: (l, i) + (0,) * len(s))
    in_specs = [tok(GDN_CONV_CH), tok(AB_W), tok(GDN_V), tok(SC_W),
                _lspec(l, (GDN_CONV, GDN_CONV_CH)), _lspec(l, (SC_CONV, SC_WIDTH)),
                _lspec(l, (SUBLANES, LANES)), _lspec(l, (1, GDN_DV))]
    args = [qkv, ab, z, sc, p['w_gdn_conv'], p['w_sc_conv'], p['hp'], p['gdn_norm_g']]
    if states is not None:
        in_specs += [per_lb(GDN_HEADS, GDN_DK, GDN_DV), per_lb(GDN_CONV - 1, GDN_CONV_CH),
                     per_lb(SC_CONV - 1, SC_WIDTH)]
        args += list(states)
    return pl.pallas_call(
        functools.partial(_gdn_kernel, bb=bb, C=C, has_state=states is not None),
        grid=(B // bb, T // C),
        in_specs=in_specs,
        out_specs=[tok(GDN_V), tok(SC_WIDTH),
                   per_b(GDN_HEADS, GDN_DK, GDN_DV), per_b(SC_CONV - 1, SC_WIDTH)],
        out_shape=[jax.ShapeDtypeStruct((B, T, GDN_V), F32),
                   jax.ShapeDtypeStruct((B, T, SC_WIDTH), F32),
                   jax.ShapeDtypeStruct((B, GDN_HEADS, GDN_DK, GDN_DV), F32),
                   jax.ShapeDtypeStruct((B, SC_CONV - 1, SC_WIDTH), F32)],
        scratch_shapes=[pltpu.VMEM((bb, GDN_HEADS, GDN_DK, GDN_DV), F32),
                        pltpu.VMEM((bb, SUBLANES + C, GDN_CONV_CH), F32),
                        pltpu.VMEM((bb, SUBLANES + C, SC_WIDTH), F32)],
        compiler_params=_params(("arbitrary", "arbitrary")),
        name="gdn",
    )(*args)


def _swa_norm_k(x, gk):
    return jnp.concatenate(
        [_rms(x[:, SWA_Q + i * SWA_HD:SWA_Q + (i + 1) * SWA_HD], gk) for i in range(SWA_KV_HEADS)],
        axis=-1)


def _swa_norm_q(x, gq):
    return [_rms(x[:, h * SWA_HD:(h + 1) * SWA_HD], gq) for h in range(SWA_HEADS)]


def _group_queries(qn, g, r0, r1, sinks):
    heads = range(g * SWA_GROUP, (g + 1) * SWA_GROUP)
    q = jnp.concatenate([qn[h][r0:r1] for h in heads], axis=0)
    sink = jnp.concatenate(
        [jnp.broadcast_to(sinks[:, h:h + 1], (r1 - r0, 1)) for h in heads], axis=0)
    return q, sink


def _attend_many(problems):
    scores = [_dot_nt(q, k) * (SWA_HD ** -0.5) for q, k, _, _, _ in problems]
    probs = []
    for s, (_, _, _, valid, sink) in zip(scores, problems):
        if valid is not None:
            s = jnp.where(valid, s, NEG_INF)
        m = jnp.maximum(jnp.max(s, axis=-1, keepdims=True), sink)
        p = jnp.exp(s - m)
        probs.append(p / (jnp.sum(p, axis=-1, keepdims=True) + jnp.exp(sink - m)))
    return [_dot(p, pr[2]) for p, pr in zip(probs, problems)]


def _ungroup(outs, rows):
    return jnp.concatenate(
        [o[j * rows:(j + 1) * rows] for o in outs for j in range(SWA_GROUP)], axis=-1)


def _swa_prompt_kernel(x_ref, halo_ref, gq_ref, gk_ref, sink_ref, o_ref, kn_ref, *, qb):
    i = pl.program_id(1)
    band = (WIN_CHUNKS + 1) * CHUNK
    n_chunks = qb // CHUNK
    x = x_ref[...]
    hx = halo_ref[...]
    kn = _swa_norm_k(x, gk_ref[...])
    kn_ref[...] = kn
    k_all = jnp.concatenate([_swa_norm_k(hx, gk_ref[...]), kn], axis=0)
    v_all = jnp.concatenate([hx[:, SWA_Q + SWA_KV:SWA_W], x[:, SWA_Q + SWA_KV:SWA_W]], axis=0)
    qn = _swa_norm_q(x, gq_ref[...])
    local_chunk = lax.broadcasted_iota(jnp.int32, (1, band), 1) // CHUNK
    problems = []
    for c in range(n_chunks):
        valid = local_chunk + (i * n_chunks + c - WIN_CHUNKS) >= 0
        for g in range(SWA_KV_HEADS):
            q, sink = _group_queries(qn, g, c * CHUNK, (c + 1) * CHUNK, sink_ref[...])
            cols = slice(g * SWA_HD, (g + 1) * SWA_HD)
            problems.append((q, k_all[c * CHUNK:c * CHUNK + band, cols],
                             v_all[c * CHUNK:c * CHUNK + band, cols], valid, sink))
    outs = _attend_many(problems)
    for c in range(n_chunks):
        o_ref[c * CHUNK:(c + 1) * CHUNK, :] = _ungroup(
            outs[c * SWA_KV_HEADS:(c + 1) * SWA_KV_HEADS], CHUNK)


def _swa_prompt(swa, l, p):
    B, T, _ = swa.shape
    qb = min(SWA_QUERY_BLOCK, T)
    assert T % qb == 0 and qb % WINDOW == 0
    halo_per_block = qb // WINDOW
    return pl.pallas_call(
        functools.partial(_swa_prompt_kernel, qb=qb),
        grid=(B, T // qb),
        in_specs=[pl.BlockSpec((None, qb, SWA_W), lambda b, i: (b, i, 0)),
                  pl.BlockSpec((None, WINDOW, SWA_W),
                               lambda b, i: (b, jnp.maximum(i * halo_per_block - 1, 0), 0)),
                  _lspec(l, (1, SWA_HD)), _lspec(l, (1, SWA_HD)), _lspec(l, (1, LANES))],
        out_specs=[pl.BlockSpec((None, qb, SWA_Q), lambda b, i: (b, i, 0)),
                   pl.BlockSpec((None, qb, SWA_KV), lambda b, i: (b, i, 0))],
        out_shape=[jax.ShapeDtypeStruct((B, T, SWA_Q), F32),
                   jax.ShapeDtypeStruct((B, T, SWA_KV), F32)],
        compiler_params=_params(("arbitrary", "arbitrary")),
        name="swa_prompt",
    )(swa, swa, p['swa_q_norm_g'], p['swa_k_norm_g'], p['swa_sinks'])


def _swa_sample_kernel(x_ref, kp_ref, vp_ref, gq_ref, gk_ref, sink_ref, o_ref, kn_ref, *, bb, T):
    problems = []
    for b in range(bb):
        x = x_ref[b]
        kn = _swa_norm_k(x, gk_ref[...])
        kn_ref[b] = kn
        k_all = jnp.concatenate([kp_ref[b], kn], axis=0)
        v_all = jnp.concatenate([vp_ref[b], x[:, SWA_Q + SWA_KV:SWA_W]], axis=0)
        qn = _swa_norm_q(x, gq_ref[...])
        for g in range(SWA_KV_HEADS):
            q, sink = _group_queries(qn, g, 0, T, sink_ref[...])
            cols = slice(g * SWA_HD, (g + 1) * SWA_HD)
            problems.append((q, k_all[:, cols], v_all[:, cols], None, sink))
    outs = _attend_many(problems)
    for b in range(bb):
        o_ref[b] = _ungroup(outs[b * SWA_KV_HEADS:(b + 1) * SWA_KV_HEADS], T)


def _swa_sample(swa, k_past, v_past, l, p):
    B, T, _ = swa.shape
    past = k_past.shape[2]
    bb = 4 if B % 4 == 0 else 1
    blk = lambda r, w: pl.BlockSpec((bb, r, w), lambda b: (b, 0, 0))
    cache = pl.BlockSpec((None, bb, past, SWA_KV), lambda b: (l, b, 0, 0))
    return pl.pallas_call(
        functools.partial(_swa_sample_kernel, bb=bb, T=T),
        grid=(B // bb,),
        in_specs=[blk(T, SWA_W), cache, cache,
                  _lspec(l, (1, SWA_HD)), _lspec(l, (1, SWA_HD)), _lspec(l, (1, LANES))],
        out_specs=[blk(T, SWA_Q), blk(T, SWA_KV)],
        out_shape=[jax.ShapeDtypeStruct((B, T, SWA_Q), F32),
                   jax.ShapeDtypeStruct((B, T, SWA_KV), F32)],
        compiler_params=_params(("arbitrary",)),
        name="swa_sample",
    )(swa, k_past, v_past, p['swa_q_norm_g'], p['swa_k_norm_g'], p['swa_sinks'])


def _mem_kv_kernel(m_ref, g_ref, wk_ref, wv_ref, gk_ref, k_ref, v_ref):
    m = _rms(m_ref[...], g_ref[...]).astype(BF16)
    k = jnp.dot(m, wk_ref[...], preferred_element_type=F32)
    k_ref[...] = jnp.concatenate(
        [_rms(k[:, h * MEM_HD:(h + 1) * MEM_HD], gk_ref[...]) for h in range(MEM_HEADS)], axis=-1)
    v_ref[...] = jnp.dot(m, wv_ref[...], preferred_element_type=F32)


def _mem_kv(mem, l, p):
    B, M, _ = mem.shape
    blk = pl.BlockSpec((None, M, D_MODEL), lambda b: (b, 0, 0))
    return pl.pallas_call(
        _mem_kv_kernel,
        grid=(B,),
        in_specs=[blk, _lspec(l, (1, D_MODEL)), _lspec(l, (D_MODEL, D_MODEL)),
                  _lspec(l, (D_MODEL, D_MODEL)), _lspec(l, (1, MEM_HD))],
        out_specs=[blk, blk],
        out_shape=[jax.ShapeDtypeStruct((B, M, D_MODEL), F32)] * 2,
        compiler_params=_params(("arbitrary",)),
        name="mem_kv",
    )(mem, p['mem_in_norm_g'], p['w_mk'], p['w_mv'], p['mem_k_norm_g'])


def _rows_of(ref, nb):
    return ref[0] if nb == 1 else jnp.concatenate([ref[b] for b in range(nb)], axis=0)


def _post_kernel(x_ref, a_ref, b_ref, c_ref, wo_ref, gm_ref, wq_ref, gq_ref, mk_ref, mv_ref,
                 wmo_ref, o_ref, *, nb, tm):
    h = _rows_from_slabs(x_ref, nb)
    h = h + (jnp.dot(_rows_of(a_ref, nb).astype(BF16), wo_ref[0:GDN_V, :],
                     preferred_element_type=F32)
             + jnp.dot(_rows_of(b_ref, nb).astype(BF16), wo_ref[GDN_V:GDN_V + SWA_Q, :],
                       preferred_element_type=F32)
             + jnp.dot(_rows_of(c_ref, nb).astype(BF16), wo_ref[GDN_V + SWA_Q:, :],
                       preferred_element_type=F32))
    hn = _rms(h, gm_ref[...]).astype(BF16)
    q = jnp.dot(hn, wq_ref[...], preferred_element_type=F32)
    probs = [(b, slice(hd * MEM_HD, (hd + 1) * MEM_HD)) for b in range(nb) for hd in range(MEM_HEADS)]
    qhs = [_rms(q[b * tm:(b + 1) * tm, sl], gq_ref[...]) for b, sl in probs]
    scores = [_dot_nt(qh, mk_ref[b, :, sl]) * (MEM_HD ** -0.5) for qh, (b, sl) in zip(qhs, probs)]
    ps = []
    for s in scores:
        p = jnp.exp(s - jnp.max(s, axis=-1, keepdims=True))
        ps.append(p / jnp.sum(p, axis=-1, keepdims=True))
    outs = [_dot(p, mv_ref[b, :, sl]) for p, (b, sl) in zip(ps, probs)]
    o = jnp.concatenate(
        [jnp.concatenate(outs[b * MEM_HEADS:(b + 1) * MEM_HEADS], axis=-1) for b in range(nb)],
        axis=0).astype(BF16)
    _rows_to_slabs(o_ref, h + jnp.dot(o, wmo_ref[...], preferred_element_type=F32), nb, tm)


def _post(xs, oa, ob, oc, mk, mv, lm, l, p):
    B, _, T, _ = xs.shape
    nb, tm = _seq_tiles(B, T)
    tok = lambda w: pl.BlockSpec((nb, tm, w), lambda b, i: (b, i, 0))
    mem = pl.BlockSpec((None, nb, MEM_TOKENS, D_MODEL), lambda b, i: (lm, b, 0, 0))
    sq = _lspec(l, (D_MODEL, D_MODEL))
    return pl.pallas_call(
        functools.partial(_post_kernel, nb=nb, tm=tm),
        grid=(B // nb, T // tm),
        in_specs=[_slab_spec(nb, tm), tok(GDN_V), tok(SWA_Q), tok(SC_WIDTH), sq,
                  _lspec(l, (1, D_MODEL)), sq, _lspec(l, (1, MEM_HD)), mem, mem, sq],
        out_specs=_slab_spec(nb, tm),
        out_shape=jax.ShapeDtypeStruct(xs.shape, F32),
        compiler_params=_params(("arbitrary", "arbitrary")),
        name="post",
    )(xs, oa, ob, oc, p['w_o'], p['norm_mem_g'], p['w_mq'], p['mem_q_norm_g'], mk, mv, p['w_mo'])


def _ffn_kernel(*refs, nb, tm, n_sub, has_state):
    h_ref, g_ref, wup_ref, wc_ref, wdn_ref = refs[:5]
    buf_ref = refs[5] if has_state else None
    o_ref, st_ref, carry_scr = refs[5 + int(has_state):]
    i = pl.program_id(1)
    ts = tm // n_sub
    L = ts // SUBLANES
    tail = 2 * SUBLANES

    @pl.when(i == 0)
    def _init():
        carry_scr[...] = jnp.zeros_like(carry_scr)
        if has_state:
            for b in range(nb):
                carry_scr[b, SUBLANES - 1:SUBLANES, :] = buf_ref[b, 0:1, :]
                carry_scr[b, tail - 1:tail, :] = buf_ref[b, 1:2, :]

    def conv_act(xn, lo, hi):
        cw = hi - lo
        last_sub = lax.broadcasted_iota(jnp.int32, (SUBLANES, cw), 0) == SUBLANES - 1
        conv = []
        for base in (0, D_FF):
            cols = slice(base + lo, base + hi)
            u = jnp.dot(xn, wup_ref[:, cols], preferred_element_type=F32)
            prev1, prev2 = [], []
            for b in range(nb):
                ub = u[b * ts:(b + 1) * ts]
                grp_a = jnp.where(last_sub, carry_scr[b, 0:SUBLANES, cols],
                                  ub[ts - tail:ts - SUBLANES])
                grp_b = jnp.where(last_sub, carry_scr[b, SUBLANES:tail, cols],
                                  ub[ts - SUBLANES:ts])
                fix_a = pltpu.roll(grp_a, 1, axis=0)
                fix_b = pltpu.roll(grp_b, 1, axis=0)
                prev1 += [fix_b, ub[0:ts - SUBLANES]]
                prev2 += [fix_a, fix_b] + ([ub[0:ts - tail]] if ts > tail else [])
                carry_scr[b, :, cols] = ub[ts - tail:ts]
            y = jnp.concatenate(prev2, axis=0) * wc_ref[0:1, cols]
            y = y + jnp.concatenate(prev1, axis=0) * wc_ref[1:2, cols]
            y = y + u * wc_ref[2:3, cols]
            conv.append(y)
        return (_silu(conv[0]) * conv[1]).astype(BF16)

    def down(acc, act, lo, hi):
        return acc + jnp.dot(act, wdn_ref[lo:hi, :], preferred_element_type=F32)

    n_ch = len(FFN_COL_CHUNKS)
    for sub in range(n_sub):
        t0 = sub * ts
        hs = jnp.concatenate(
            [jnp.concatenate([h_ref[b, c, pl.ds(t0 + j, SUBLANES, stride=L), :]
                              for b in range(nb) for j in range(L)], axis=0)
             for c in range(N_SLABS)], axis=-1)
        xn = _rms(hs, g_ref[...]).astype(BF16)
        acc = hs
        act = conv_act(xn, *FFN_COL_CHUNKS[0])
        for k in range(n_ch):
            nxt = conv_act(xn, *FFN_COL_CHUNKS[k + 1]) if k + 1 < n_ch else None
            acc = down(acc, act, *FFN_COL_CHUNKS[k])
            act = nxt
        for b in range(nb):
            for j in range(L):
                r = (b * L + j) * SUBLANES
                for c in range(N_SLABS):
                    o_ref[b, c, pl.ds(t0 + j, SUBLANES, stride=L), :] = acc[
                        r:r + SUBLANES, c * LANES:(c + 1) * LANES]

    @pl.when(i == pl.num_programs(1) - 1)
    def _fin():
        for b in range(nb):
            st_ref[b, 0:1, :] = carry_scr[b, SUBLANES - 1:SUBLANES, :]
            st_ref[b, 1:2, :] = carry_scr[b, tail - 1:tail, :]


def _ffn(h, l, p, state):
    B, _, T, _ = h.shape
    nb, tm = _seq_tiles(B, T)
    n_sub = 1
    if nb == 1 and T % (FFN_SUB_TILES * tm) == 0:
        n_sub, tm = FFN_SUB_TILES, FFN_SUB_TILES * tm
    assert (tm // n_sub) % (2 * SUBLANES) == 0
    tok = _slab_spec(nb, tm)
    in_specs = [tok, _lspec(l, (1, D_MODEL)), _lspec(l, (D_MODEL, 2 * D_FF), single_buffer=True),
                _lspec(l, (FFN_CONV, 2 * D_FF)), _lspec(l, (D_FF, D_MODEL), single_buffer=True)]
    args = [h, p['norm_ffn_g'], p['w_up'], p['w_ffn_conv'], p['w_down']]
    if state is not None:
        in_specs.append(pl.BlockSpec((None, nb, FFN_CONV - 1, 2 * D_FF), lambda b, i: (l, b, 0, 0)))
        args.append(state)
    return pl.pallas_call(
        functools.partial(_ffn_kernel, nb=nb, tm=tm, n_sub=n_sub, has_state=state is not None),
        grid=(B // nb, T // tm),
        in_specs=in_specs,
        out_specs=[tok, pl.BlockSpec((nb, FFN_CONV - 1, 2 * D_FF), lambda b, i: (b, 0, 0))],
        out_shape=[jax.ShapeDtypeStruct(h.shape, F32),
                   jax.ShapeDtypeStruct((B, FFN_CONV - 1, 2 * D_FF), F32)],
        scratch_shapes=[pltpu.VMEM((nb, 2 * SUBLANES, 2 * D_FF), F32)],
        compiler_params=_params(("arbitrary", "arbitrary")),
        name="ffn",
    )(*args)


def _layer(x, l, p, mk, mv, lm, gdn_states, ffn_state, swa_cache):
    B, _, T, _ = x.shape
    qkv, z, swa, sc, ab = _in_proj(x, l, p)
    oa, oc, s_new, sc_new = _gdn(qkv, ab, z, sc, l, p, gdn_states)
    if swa_cache is None:
        ob, kn = _swa_prompt(swa, l, p)
    else:
        ob, kn = _swa_sample(swa, swa_cache[0], swa_cache[1], l, p)
    h = _post(x, oa, ob, oc, mk, mv, lm, l, p)
    h, f_new = _ffn(h, l, p, ffn_state)
    gb_new = qkv[:, T - (GDN_CONV - 1):, :]
    sk = kn.reshape(B, T, SWA_KV_HEADS, SWA_HD)
    sv = swa[:, :, SWA_Q + SWA_KV:].reshape(B, T, SWA_KV_HEADS, SWA_HD)
    return h, s_new, gb_new, sc_new, f_new, sk, sv


def kernel(x_prompt, x_sample, mem_prompt, state_gdn, state_gdn_conv, cache_swa_k, cache_swa_v, state_sc_conv, cache_mem_k, cache_mem_v, state_ffn_conv, norm_mix_g, w_in, w_gdn_conv, gdn_a_log, gdn_dt_bias, gdn_norm_g, swa_q_norm_g, swa_k_norm_g, swa_sinks, w_sc_conv, w_o, norm_mem_g, mem_in_norm_g, w_mq, w_mk, w_mv, mem_q_norm_g, mem_k_norm_g, w_mo, norm_ffn_g, w_up, w_ffn_conv, w_down):
    depth = w_in.shape[0]
    Bp, Tp, _ = x_prompt.shape
    Bs, Ts, _ = x_sample.shape
    keep = min(WINDOW, Tp)
    past = cache_swa_k.shape[2]

    qz = GDN_CONV_CH + GDN_V
    w_in_b = w_in.astype(BF16)
    w_in_r = jnp.concatenate(
        [w_in_b[:, :, :qz], w_in_b[:, :, qz + 2 * GDN_HEADS:], w_in_b[:, :, qz:qz + 2 * GDN_HEADS],
         jnp.zeros((depth, D_MODEL, AB_W - 2 * GDN_HEADS), BF16)], axis=-1)
    hp = jnp.zeros((depth, SUBLANES, LANES), F32)
    hp = hp.at[:, 0, :GDN_HEADS].set(gdn_a_log).at[:, 1, :GDN_HEADS].set(gdn_dt_bias)
    sinks = jnp.zeros((depth, 1, LANES), F32).at[:, 0, :SWA_HEADS].set(swa_sinks)
    row = lambda a: a[:, None, :]
    p = {
        'norm_mix_g': row(norm_mix_g), 'w_in': w_in_r, 'w_gdn_conv': w_gdn_conv, 'hp': hp,
        'gdn_norm_g': row(gdn_norm_g), 'swa_q_norm_g': row(swa_q_norm_g),
        'swa_k_norm_g': row(swa_k_norm_g), 'swa_sinks': sinks, 'w_sc_conv': w_sc_conv,
        'w_o': w_o.astype(BF16), 'norm_mem_g': row(norm_mem_g), 'mem_in_norm_g': row(mem_in_norm_g),
        'w_mq': w_mq.astype(BF16), 'w_mk': w_mk.astype(BF16), 'w_mv': w_mv.astype(BF16),
        'mem_q_norm_g': row(mem_q_norm_g), 'mem_k_norm_g': row(mem_k_norm_g),
        'w_mo': w_mo.astype(BF16), 'norm_ffn_g': row(norm_ffn_g), 'w_up': w_up.astype(BF16),
        'w_ffn_conv': w_ffn_conv, 'w_down': w_down.astype(BF16),
    }
    cache_k = cache_swa_k.reshape(depth, Bs, past, SWA_KV)
    cache_v = cache_swa_v.reshape(depth, Bs, past, SWA_KV)
    mem_k_s = cache_mem_k.reshape(depth, Bs, MEM_TOKENS, D_MODEL)
    mem_v_s = cache_mem_v.reshape(depth, Bs, MEM_TOKENS, D_MODEL)

    hp_, hs_ = _to_slabs(x_prompt), _to_slabs(x_sample)
    outs_p = {k: [] for k in ('S', 'gb', 'sk', 'sv', 'sb', 'mk', 'mv', 'fb')}
    outs_s = {k: [] for k in ('S', 'gb', 'sk', 'sv', 'sb', 'fb')}
    for l in range(depth):
        mk, mv = _mem_kv(mem_prompt, l, p)
        hp_, S, gb, sb, fb, sk, sv = _layer(hp_, l, p, mk[None], mv[None], 0, None, None, None)
        outs_p['S'].append(S); outs_p['gb'].append(gb); outs_p['sb'].append(sb); outs_p['fb'].append(fb)
        outs_p['sk'].append(sk[:, Tp - keep:]); outs_p['sv'].append(sv[:, Tp - keep:])
        outs_p['mk'].append(mk.reshape(Bp, MEM_TOKENS, MEM_HEADS, MEM_HD))
        outs_p['mv'].append(mv.reshape(Bp, MEM_TOKENS, MEM_HEADS, MEM_HD))
        hs_, S, gb, sb, fb, sk, sv = _layer(
            hs_, l, p, mem_k_s, mem_v_s, l, (state_gdn, state_gdn_conv, state_sc_conv),
            state_ffn_conv, (cache_k, cache_v))
        outs_s['S'].append(S); outs_s['gb'].append(gb); outs_s['sb'].append(sb); outs_s['fb'].append(fb)
        outs_s['sk'].append(sk); outs_s['sv'].append(sv)
    st = jnp.stack
    return (_from_slabs(hp_), _from_slabs(hs_),
            st(outs_p['S']), st(outs_p['gb']), st(outs_p['sk']), st(outs_p['sv']), st(outs_p['sb']),
            st(outs_p['mk']), st(outs_p['mv']), st(outs_p['fb']),
            st(outs_s['S']), st(outs_s['gb']), st(outs_s['sk']), st(outs_s['sv']), st(outs_s['sb']),
            st(outs_s['fb']))
```

```python
import functools

import jax
import jax.numpy as jnp
from jax import lax
from jax.experimental import pallas as pl
from jax.experimental.pallas import tpu as pltpu

F32 = jnp.float32
BF16 = jnp.bfloat16

D_MODEL = 1024
CHUNK = 64
EPS = 1e-6
NEG_INF = -1e30
GDN_HEADS = 4
GDN_DK = 128
GDN_DV = 128
GDN_CONV = 4
GDN_QK = GDN_HEADS * GDN_DK
GDN_V = GDN_HEADS * GDN_DV
GDN_CONV_CH = 2 * GDN_QK + GDN_V
SWA_HEADS = 4
SWA_KV_HEADS = 2
SWA_GROUP = SWA_HEADS // SWA_KV_HEADS
SWA_HD = 64
WINDOW = 128
WIN_CHUNKS = WINDOW // CHUNK
SWA_Q = SWA_HEADS * SWA_HD
SWA_KV = SWA_KV_HEADS * SWA_HD
SWA_W = SWA_Q + 2 * SWA_KV
assert SWA_GROUP == SWA_KV_HEADS == 2 and SWA_KV == 128
SC_WIDTH = 256
SC_CONV = 3
MEM_TOKENS = 256
MEM_HEADS = 4
MEM_HD = D_MODEL // MEM_HEADS
D_FF = 2816
FFN_CONV = 3
LANES = 128
SUBLANES = 8
N_SLABS = D_MODEL // LANES
AB_W = LANES
SC_W = 3 * SC_WIDTH
IN_COLS = (
    (0, GDN_CONV_CH),
    (GDN_CONV_CH, GDN_CONV_CH + GDN_V),
    (GDN_CONV_CH + GDN_V, GDN_CONV_CH + GDN_V + SWA_W),
    (GDN_CONV_CH + GDN_V + SWA_W, GDN_CONV_CH + GDN_V + SWA_W + SC_W),
    (GDN_CONV_CH + GDN_V + SWA_W + SC_W, GDN_CONV_CH + GDN_V + SWA_W + SC_W + AB_W),
)
D_IN_PAD = IN_COLS[-1][1]
FFN_COL_CHUNKS = ((0, 1024), (1024, 2048), (2048, D_FF))
GDN_CHUNKS_PER_STEP = 4
FFN_SUB_TILES = 2
VMEM_LIMIT = 56 * 1024 * 1024
TOKEN_TILE = 512
SWA_QUERY_BLOCK = 512
MAX_ROWS_SMALL_SEQ = 128


def _seq_tiles(B, T):
    if T >= TOKEN_TILE:
        return 1, TOKEN_TILE
    nb = max(1, min(B, MAX_ROWS_SMALL_SEQ // T))
    while B % nb:
        nb -= 1
    return nb, T


def _dot(a, b):
    return jnp.dot(a.astype(BF16), b.astype(BF16), preferred_element_type=F32)


def _dot_nt(a, b):
    return lax.dot_general(a.astype(BF16), b.astype(BF16), (((1,), (1,)), ((), ())),
                           preferred_element_type=F32)


def _dot_tn(a, b):
    return lax.dot_general(a.astype(BF16), b.astype(BF16), (((0,), (0,)), ((), ())),
                           preferred_element_type=F32)


def _rms(x, g):
    return x * lax.rsqrt(jnp.mean(x * x, axis=-1, keepdims=True) + EPS) * g


def _silu(x):
    return x * jax.nn.sigmoid(x)


def _lspec(l, shape, single_buffer=False):
    nd = len(shape)
    mode = dict(pipeline_mode=pl.Buffered(1)) if single_buffer else {}
    return pl.BlockSpec((None,) + tuple(shape), lambda *_: (l,) + (0,) * nd, **mode)


def _params(sem):
    return pltpu.CompilerParams(dimension_semantics=sem, vmem_limit_bytes=VMEM_LIMIT)


def _slab_spec(nb, tm):
    return pl.BlockSpec((nb, N_SLABS, tm, LANES), lambda b, i: (b, 0, i, 0))


def _stream_spec(x, nb, tm):
    if x.ndim == 4:
        return _slab_spec(nb, tm)
    return pl.BlockSpec((nb, tm, D_MODEL), lambda b, i: (b, i, 0))


def _rows_from_slabs(ref, nb):
    if len(ref.shape) == 3:
        seqs = [ref[b] for b in range(nb)]
    else:
        seqs = [jnp.concatenate([ref[b, c] for c in range(N_SLABS)], axis=-1) for b in range(nb)]
    return seqs[0] if nb == 1 else jnp.concatenate(seqs, axis=0)


def _rows_to_slabs(ref, val, nb, tm):
    for b in range(nb):
        for c in range(N_SLABS):
            ref[b, c] = val[b * tm:(b + 1) * tm, c * LANES:(c + 1) * LANES]


def _in_proj_kernel(x_ref, g_ref, w_ref, qkv_ref, z_ref, swa_ref, sc_ref, ab_ref, *, nb, tm):
    xn = _rms(_rows_from_slabs(x_ref, nb), g_ref[...]).astype(BF16)
    for o_ref, (lo, hi) in zip((qkv_ref, z_ref, swa_ref, sc_ref, ab_ref), IN_COLS):
        res = jnp.dot(xn, w_ref[:, lo:hi], preferred_element_type=F32)
        for b in range(nb):
            o_ref[b] = res[b * tm:(b + 1) * tm]


def _stream_dims(x):
    return (x.shape[0], x.shape[2]) if x.ndim == 4 else x.shape[:2]


def _in_proj(xs, l, p):
    B, T = _stream_dims(xs)
    nb, tm = _seq_tiles(B, T)
    widths = [hi - lo for lo, hi in IN_COLS]
    return pl.pallas_call(
        functools.partial(_in_proj_kernel, nb=nb, tm=tm),
        grid=(B // nb, T // tm),
        in_specs=[_stream_spec(xs, nb, tm), _lspec(l, (1, D_MODEL)),
                  _lspec(l, (D_MODEL, D_IN_PAD))],
        out_specs=[pl.BlockSpec((nb, tm, wd), lambda b, i: (b, i, 0)) for wd in widths],
        out_shape=[jax.ShapeDtypeStruct((B, T, wd), F32) for wd in widths],
        compiler_params=_params(("arbitrary", "arbitrary")),
        name="in_proj",
    )(xs, p['norm_mix_g'], p['w_in'])


def _cumsum_rows(x):
    rows = x.shape[0]
    row = lax.broadcasted_iota(jnp.int32, x.shape, 0)
    s = 1
    while s < rows:
        x = x + jnp.where(row >= s, pltpu.roll(x, s, axis=0), 0.0)
        s *= 2
    return x


def _gdn_kernel(*refs, bb, C, nc, has_state):
    qkv_ref, ab_ref, z_ref, sc_ref, wconv_ref, wsc_ref, hp_ref, gn_ref = refs[:8]
    state_refs = refs[8:11] if has_state else ()
    oa_ref, oc_ref, sout_ref, scout_ref, s_scr, xp_scr, scp_scr = refs[8 + len(state_refs):]
    n = pl.program_id(1)
    pad = SUBLANES

    @pl.when(n == 0)
    def _init():
        xp_scr[...] = jnp.zeros_like(xp_scr)
        scp_scr[...] = jnp.zeros_like(scp_scr)
        if has_state:
            s0_ref, gbuf_ref, scbuf_ref = state_refs
            s_scr[...] = s0_ref[...]
            xp_scr[:, pad - (GDN_CONV - 1):pad, :] = gbuf_ref[...]
            scp_scr[:, pad - (SC_CONV - 1):pad, :] = scbuf_ref[...]
        else:
            s_scr[...] = jnp.zeros_like(s_scr)

    row_i = lax.broadcasted_iota(jnp.int32, (C, C), 0)
    col_j = lax.broadcasted_iota(jnp.int32, (C, C), 1)
    incl = row_i >= col_j
    strict = row_i > col_j
    neg_a = -jnp.exp(hp_ref[0:1, :])
    dt_b = hp_ref[1:2, :]
    n_sq = C.bit_length() - 2

    R = nc * C
    ys, gcs, gcts, betas = [], [], [], []
    for b in range(bb):
        x = qkv_ref[b]
        ext = jnp.concatenate([xp_scr[b], x], axis=0)
        y = pltpu.roll(ext, 3, axis=0)[pad:] * wconv_ref[0:1, :]
        y = y + pltpu.roll(ext, 2, axis=0)[pad:] * wconv_ref[1:2, :]
        y = y + pltpu.roll(ext, 1, axis=0)[pad:] * wconv_ref[2:3, :]
        y = y + x * wconv_ref[3:4, :]
        xp_scr[b] = x[R - pad:R, :]
        ys.append(_silu(y))

        ab = ab_ref[b]
        t = ab + dt_b
        g = neg_a * (jnp.maximum(t, 0.0) + jnp.log1p(jnp.exp(-jnp.abs(t))))
        betas.append(jax.nn.sigmoid(ab))
        gc_b = [_cumsum_rows(g[c * C:(c + 1) * C]) for c in range(nc)]
        gcs.append(gc_b)
        gcts.append([jnp.transpose(jnp.concatenate(
            [gc, jnp.zeros((LANES - C, LANES), F32)], axis=0)) for gc in gc_b])

        scv = sc_ref[b]
        prod = scv[:, SC_WIDTH:2 * SC_WIDTH] * scv[:, 2 * SC_WIDTH:3 * SC_WIDTH]
        pext = jnp.concatenate([scp_scr[b], prod], axis=0)
        cu = pltpu.roll(pext, 2, axis=0)[pad:] * wsc_ref[0:1, :]
        cu = cu + pltpu.roll(pext, 1, axis=0)[pad:] * wsc_ref[1:2, :]
        cu = cu + prod * wsc_ref[2:3, :]
        scp_scr[b] = prod[R - pad:R, :]
        oc_ref[b] = scv[:, 0:SC_WIDTH] * cu

    problems = [(c, b, h) for c in range(nc) for b in range(bb) for h in range(GDN_HEADS)]
    every = range(len(problems))
    qs, ks, vs, kbs, bts, egs, e_incls, e_stricts, kds, gls = ([] for _ in range(10))
    for c, b, h in problems:
        rows = slice(c * C, (c + 1) * C)
        y, gc = ys[b], gcs[b][c]
        q = y[rows, h * GDN_DK:(h + 1) * GDN_DK]
        k = y[rows, GDN_QK + h * GDN_DK:GDN_QK + (h + 1) * GDN_DK]
        q = q * lax.rsqrt(jnp.sum(q * q, axis=-1, keepdims=True) + EPS) * (GDN_DK ** -0.5)
        k = k * lax.rsqrt(jnp.sum(k * k, axis=-1, keepdims=True) + EPS)
        gci = gc[:, h:h + 1]
        gcj = gcts[b][c][h:h + 1, 0:C]
        g_last = gc[C - 1:C, h:h + 1]
        e_incl = jnp.exp(jnp.where(incl, gci - gcj, NEG_INF))
        qs.append(q)
        ks.append(k)
        kbs.append(k.astype(BF16))
        vs.append(y[rows, 2 * GDN_QK + h * GDN_DV:2 * GDN_QK + (h + 1) * GDN_DV])
        bts.append(betas[b][rows, GDN_HEADS + h:GDN_HEADS + h + 1])
        egs.append(jnp.exp(gci))
        e_incls.append(e_incl)
        e_stricts.append(jnp.where(strict, e_incl, 0.0))
        kds.append(k * jnp.exp(g_last - gci))
        gls.append(jnp.exp(g_last))

    kks = [_dot_nt(kbs[i], kbs[i]) for i in every]
    qks = [_dot_nt(qs[i], kbs[i]) * e_incls[i] for i in every]
    p_pows = [-(bts[i] * kks[i] * e_stricts[i]) for i in every]
    n_accs = list(p_pows)
    for _ in range(n_sq):
        pbs = [p.astype(BF16) for p in p_pows]
        p_pows = [_dot(pb, pb) for pb in pbs]
        n_accs = [n_accs[i] + p_pows[i] + _dot(p_pows[i], n_accs[i]) for i in every]
    rhss = [jnp.concatenate([bts[i] * vs[i], (bts[i] * egs[i]) * ks[i]], axis=-1) for i in every]
    sols = [rhss[i] + _dot(n_accs[i], rhss[i]) for i in every]

    pairs = [(b, h) for b in range(bb) for h in range(GDN_HEADS)]
    here = range(len(pairs))
    states = [s_scr[b, h] for b, h in pairs]
    for c in range(nc):
        base = c * len(pairs)
        sbs = [s.astype(BF16) for s in states]
        o_inter = [_dot(qs[base + j] * egs[base + j], sbs[j]) for j in here]
        us = [sols[base + j][:, :GDN_DV] - _dot(sols[base + j][:, GDN_DV:], sbs[j]) for j in here]
        outs = [o_inter[j] + _dot(qks[base + j], us[j]) for j in here]
        states = [gls[base + j] * states[j] + _dot_tn(kds[base + j], us[j]) for j in here]
        for j, (b, h) in enumerate(pairs):
            cols = slice(h * GDN_DV, (h + 1) * GDN_DV)
            zh = z_ref[b, c * C:(c + 1) * C, cols]
            oa_ref[b, c * C:(c + 1) * C, cols] = _rms(outs[j], gn_ref[...]) * _silu(zh)
    for j, (b, h) in enumerate(pairs):
        s_scr[b, h] = states[j]

    @pl.when(n == pl.num_programs(1) - 1)
    def _fin():
        sout_ref[...] = s_scr[...]
        scout_ref[...] = scp_scr[:, pad - (SC_CONV - 1):pad, :]


def _gdn(qkv, ab, z, sc, l, p, states):
    B, T, _ = qkv.shape
    C = CHUNK if T % CHUNK == 0 else T
    bb = 2 if B % 2 == 0 else 1
    nc = GDN_CHUNKS_PER_STEP if (T // C) % GDN_CHUNKS_PER_STEP == 0 else 1
    tok = lambda w: pl.BlockSpec((bb, nc * C, w), lambda i, n: (i, n, 0))
    per_b = lambda *s: pl.BlockSpec((bb,) + s, lambda i, n: (i,) + (0,) * len(s))
    per_lb = lambda *s: pl.BlockSpec((None, bb) + s, lambda i, n: (l, i) + (0,) * len(s))
    in_specs = [tok(GDN_CONV_CH), tok(AB_W), tok(GDN_V), tok(SC_W),
                _lspec(l, (GDN_CONV, GDN_CONV_CH)), _lspec(l, (SC_CONV, SC_WIDTH)),
                _lspec(l, (SUBLANES, LANES)), _lspec(l, (1, GDN_DV))]
    args = [qkv, ab, z, sc, p['w_gdn_conv'], p['w_sc_conv'], p['hp'], p['gdn_norm_g']]
    if states is not None:
        in_specs += [per_lb(GDN_HEADS, GDN_DK, GDN_DV), per_lb(GDN_CONV - 1, GDN_CONV_CH),
                     per_lb(SC_CONV - 1, SC_WIDTH)]
        args += list(states)
    return pl.pallas_call(
        functools.partial(_gdn_kernel, bb=bb, C=C, nc=nc, has_state=states is not None),
        grid=(B // bb, T // (nc * C)),
        in_specs=in_specs,
        out_specs=[tok(GDN_V), tok(SC_WIDTH),
                   per_b(GDN_HEADS, GDN_DK, GDN_DV), per_b(SC_CONV - 1, SC_WIDTH)],
        out_shape=[jax.ShapeDtypeStruct((B, T, GDN_V), F32),
                   jax.ShapeDtypeStruct((B, T, SC_WIDTH), F32),
                   jax.ShapeDtypeStruct((B, GDN_HEADS, GDN_DK, GDN_DV), F32),
                   jax.ShapeDtypeStruct((B, SC_CONV - 1, SC_WIDTH), F32)],
        scratch_shapes=[pltpu.VMEM((bb, GDN_HEADS, GDN_DK, GDN_DV), F32),
                        pltpu.VMEM((bb, SUBLANES, GDN_CONV_CH), F32),
                        pltpu.VMEM((bb, SUBLANES, SC_WIDTH), F32)],
        compiler_params=_params(("arbitrary", "arbitrary")),
        name="gdn",
    )(*args)


def _swa_norm_k(x, gk):
    return jnp.concatenate(
        [_rms(x[:, SWA_Q + i * SWA_HD:SWA_Q + (i + 1) * SWA_HD], gk) for i in range(SWA_KV_HEADS)],
        axis=-1)


def _swa_norm_q(x, gq):
    return [_rms(x[:, h * SWA_HD:(h + 1) * SWA_HD], gq) for h in range(SWA_HEADS)]


def _group_queries(qn, g, r0, r1, sinks):
    heads = range(g * SWA_GROUP, (g + 1) * SWA_GROUP)
    q = jnp.concatenate([qn[h][r0:r1] for h in heads], axis=0)
    sink = jnp.concatenate(
        [jnp.broadcast_to(sinks[:, h:h + 1], (r1 - r0, 1)) for h in heads], axis=0)
    return q, sink


def _attend_many(problems):
    scores = [_dot_nt(q, k) * (SWA_HD ** -0.5) for q, k, _, _, _ in problems]
    probs = []
    for s, (_, _, _, valid, sink) in zip(scores, problems):
        if valid is not None:
            s = jnp.where(valid, s, NEG_INF)
        m = jnp.maximum(jnp.max(s, axis=-1, keepdims=True), sink)
        p = jnp.exp(s - m)
        probs.append(p / (jnp.sum(p, axis=-1, keepdims=True) + jnp.exp(sink - m)))
    return [_dot(p, pr[2]) for p, pr in zip(probs, problems)]


def _ungroup(outs, rows):
    return jnp.concatenate(
        [o[j * rows:(j + 1) * rows] for o in outs for j in range(SWA_GROUP)], axis=-1)


def _split_bf16(x):
    hi = x.astype(BF16)
    return hi, (x - hi.astype(F32)).astype(BF16)


def _sum_dot(x, ones):
    hi, lo = _split_bf16(x)
    return (jnp.dot(hi, ones, preferred_element_type=F32)
            + jnp.dot(lo, ones, preferred_element_type=F32))


def _head_rms(x, g_tiled):
    w = x.shape[1]
    same_head = (lax.broadcasted_iota(jnp.int32, (w, w), 0) // SWA_HD
                 == lax.broadcasted_iota(jnp.int32, (w, w), 1) // SWA_HD)
    ss = _sum_dot(x * x, jnp.where(same_head, 1.0, 0.0).astype(BF16))
    return x * lax.rsqrt(ss * (1.0 / SWA_HD) + EPS) * g_tiled


def _swa_prompt_kernel(x_ref, halo_ref, gq_ref, gk_ref, sink_ref, o_ref, kn_ref, *, qb):
    i = pl.program_id(1)
    band = (WIN_CHUNKS + 1) * CHUNK
    n_chunks = qb // CHUNK
    x = x_ref[...]
    hx = halo_ref[...]
    kcols = slice(SWA_Q, SWA_Q + SWA_KV)
    vcols = slice(SWA_Q + SWA_KV, SWA_W)
    qn = _head_rms(x[:, 0:SWA_Q], gq_ref[...])
    k_all = _head_rms(jnp.concatenate([hx[:, kcols], x[:, kcols]], axis=0), gk_ref[...])
    v_all = jnp.concatenate([hx[:, vcols], x[:, vcols]], axis=0)
    kn_ref[...] = k_all[WINDOW:]

    lane_half = lax.broadcasted_iota(jnp.int32, (1, SWA_KV), 1) // SWA_HD

    def variants(a):
        rot = pltpu.roll(a, SWA_HD, axis=1)
        return [[jnp.where(lane_half == j, a if j == g else rot, 0.0).astype(BF16)
                 for j in range(SWA_GROUP)] for g in range(SWA_KV_HEADS)]

    k_var, v_var = variants(k_all), variants(v_all)
    ones = jnp.ones((band, SWA_KV), BF16)
    local_chunk = lax.broadcasted_iota(jnp.int32, (1, band), 1) // CHUNK
    problems = [(c, g, j) for c in range(n_chunks) for g in range(SWA_KV_HEADS)
                for j in range(SWA_GROUP)]
    scores = []
    for c, g, j in problems:
        q = qn[c * CHUNK:(c + 1) * CHUNK, g * SWA_KV:(g + 1) * SWA_KV]
        scores.append(_dot_nt(q, k_var[g][j][c * CHUNK:c * CHUNK + band]) * (SWA_HD ** -0.5))
    ps, tails = [], []
    for s, (c, g, j) in zip(scores, problems):
        valid = local_chunk + (i * n_chunks + c - WIN_CHUNKS) >= 0
        s = jnp.where(valid, s, NEG_INF)
        sink = sink_ref[:, g * SWA_GROUP + j:g * SWA_GROUP + j + 1]
        m = jnp.maximum(jnp.max(s, axis=-1, keepdims=True), sink)
        ps.append(jnp.exp(s - m))
        tails.append(jnp.exp(sink - m))
    outs = []
    for p, tail, (c, g, j) in zip(ps, tails, problems):
        denom = _sum_dot(p, ones) + tail
        outs.append(_dot(p, v_var[g][j][c * CHUNK:c * CHUNK + band]) / denom)
    for c in range(n_chunks):
        for g in range(SWA_KV_HEADS):
            k0 = (c * SWA_KV_HEADS + g) * SWA_GROUP
            o_ref[c * CHUNK:(c + 1) * CHUNK, g * SWA_KV:(g + 1) * SWA_KV] = sum(
                outs[k0 + 1:k0 + SWA_GROUP], outs[k0])


def _swa_prompt(swa, l, p):
    B, T, _ = swa.shape
    qb = min(SWA_QUERY_BLOCK, T)
    assert T % qb == 0 and qb % WINDOW == 0
    halo_per_block = qb // WINDOW
    return pl.pallas_call(
        functools.partial(_swa_prompt_kernel, qb=qb),
        grid=(B, T // qb),
        in_specs=[pl.BlockSpec((None, qb, SWA_W), lambda b, i: (b, i, 0)),
                  pl.BlockSpec((None, WINDOW, SWA_W),
                               lambda b, i: (b, jnp.maximum(i * halo_per_block - 1, 0), 0)),
                  _lspec(l, (1, SWA_Q)), _lspec(l, (1, SWA_KV)), _lspec(l, (1, LANES))],
        out_specs=[pl.BlockSpec((None, qb, SWA_Q), lambda b, i: (b, i, 0)),
                   pl.BlockSpec((None, qb, SWA_KV), lambda b, i: (b, i, 0))],
        out_shape=[jax.ShapeDtypeStruct((B, T, SWA_Q), F32),
                   jax.ShapeDtypeStruct((B, T, SWA_KV), F32)],
        compiler_params=_params(("arbitrary", "arbitrary")),
        name="swa_prompt",
    )(swa, swa, p['swa_q_norm_g_tiled'], p['swa_k_norm_g_tiled'], p['swa_sinks'])


def _swa_sample_kernel(x_ref, kp_ref, vp_ref, gq_ref, gk_ref, sink_ref, o_ref, kn_ref, *, bb, T):
    problems = []
    for b in range(bb):
        x = x_ref[b]
        kn = _swa_norm_k(x, gk_ref[...])
        kn_ref[b] = kn
        k_all = jnp.concatenate([kp_ref[b], kn], axis=0)
        v_all = jnp.concatenate([vp_ref[b], x[:, SWA_Q + SWA_KV:SWA_W]], axis=0)
        qn = _swa_norm_q(x, gq_ref[...])
        for g in range(SWA_KV_HEADS):
            q, sink = _group_queries(qn, g, 0, T, sink_ref[...])
            cols = slice(g * SWA_HD, (g + 1) * SWA_HD)
            problems.append((q, k_all[:, cols], v_all[:, cols], None, sink))
    outs = _attend_many(problems)
    for b in range(bb):
        o_ref[b] = _ungroup(outs[b * SWA_KV_HEADS:(b + 1) * SWA_KV_HEADS], T)


def _swa_sample(swa, k_past, v_past, l, p):
    B, T, _ = swa.shape
    past = k_past.shape[2]
    bb = 4 if B % 4 == 0 else 1
    blk = lambda r, w: pl.BlockSpec((bb, r, w), lambda b: (b, 0, 0))
    cache = pl.BlockSpec((None, bb, past, SWA_KV), lambda b: (l, b, 0, 0))
    return pl.pallas_call(
        functools.partial(_swa_sample_kernel, bb=bb, T=T),
        grid=(B // bb,),
        in_specs=[blk(T, SWA_W), cache, cache,
                  _lspec(l, (1, SWA_HD)), _lspec(l, (1, SWA_HD)), _lspec(l, (1, LANES))],
        out_specs=[blk(T, SWA_Q), blk(T, SWA_KV)],
        out_shape=[jax.ShapeDtypeStruct((B, T, SWA_Q), F32),
                   jax.ShapeDtypeStruct((B, T, SWA_KV), F32)],
        compiler_params=_params(("arbitrary",)),
        name="swa_sample",
    )(swa, k_past, v_past, p['swa_q_norm_g'], p['swa_k_norm_g'], p['swa_sinks'])


def _mem_kv_kernel(m_ref, g_ref, wk_ref, wv_ref, gk_ref, k_ref, v_ref):
    m = _rms(m_ref[...], g_ref[...]).astype(BF16)
    k = jnp.dot(m, wk_ref[...], preferred_element_type=F32)
    k_ref[...] = jnp.concatenate(
        [_rms(k[:, h * MEM_HD:(h + 1) * MEM_HD], gk_ref[...]) for h in range(MEM_HEADS)], axis=-1)
    v_ref[...] = jnp.dot(m, wv_ref[...], preferred_element_type=F32)


def _mem_kv(mem, l, p):
    B, M, _ = mem.shape
    blk = pl.BlockSpec((None, M, D_MODEL), lambda b: (b, 0, 0))
    return pl.pallas_call(
        _mem_kv_kernel,
        grid=(B,),
        in_specs=[blk, _lspec(l, (1, D_MODEL)), _lspec(l, (D_MODEL, D_MODEL)),
                  _lspec(l, (D_MODEL, D_MODEL)), _lspec(l, (1, MEM_HD))],
        out_specs=[blk, blk],
        out_shape=[jax.ShapeDtypeStruct((B, M, D_MODEL), F32)] * 2,
        compiler_params=_params(("arbitrary",)),
        name="mem_kv",
    )(mem, p['mem_in_norm_g'], p['w_mk'], p['w_mv'], p['mem_k_norm_g'])


def _rows_of(ref, nb):
    return ref[0] if nb == 1 else jnp.concatenate([ref[b] for b in range(nb)], axis=0)


def _post_kernel(x_ref, a_ref, b_ref, c_ref, wo_ref, gm_ref, wq_ref, gq_ref, mk_ref, mv_ref,
                 wmo_ref, o_ref, *, nb, tm):
    h = _rows_from_slabs(x_ref, nb)
    h = h + (jnp.dot(_rows_of(a_ref, nb).astype(BF16), wo_ref[0:GDN_V, :],
                     preferred_element_type=F32)
             + jnp.dot(_rows_of(b_ref, nb).astype(BF16), wo_ref[GDN_V:GDN_V + SWA_Q, :],
                       preferred_element_type=F32)
             + jnp.dot(_rows_of(c_ref, nb).astype(BF16), wo_ref[GDN_V + SWA_Q:, :],
                       preferred_element_type=F32))
    hn = _rms(h, gm_ref[...]).astype(BF16)
    q = jnp.dot(hn, wq_ref[...], preferred_element_type=F32)
    probs = [(b, slice(hd * MEM_HD, (hd + 1) * MEM_HD)) for b in range(nb) for hd in range(MEM_HEADS)]
    qhs = [_rms(q[b * tm:(b + 1) * tm, sl], gq_ref[...]) for b, sl in probs]
    scores = [_dot_nt(qh, mk_ref[b, :, sl]) * (MEM_HD ** -0.5) for qh, (b, sl) in zip(qhs, probs)]
    ps = []
    for s in scores:
        p = jnp.exp(s - jnp.max(s, axis=-1, keepdims=True))
        ps.append(p / jnp.sum(p, axis=-1, keepdims=True))
    outs = [_dot(p, mv_ref[b, :, sl]) for p, (b, sl) in zip(ps, probs)]
    o = jnp.concatenate(
        [jnp.concatenate(outs[b * MEM_HEADS:(b + 1) * MEM_HEADS], axis=-1) for b in range(nb)],
        axis=0).astype(BF16)
    _rows_to_slabs(o_ref, h + jnp.dot(o, wmo_ref[...], preferred_element_type=F32), nb, tm)


def _post(xs, oa, ob, oc, mk, mv, lm, l, p):
    B, T = _stream_dims(xs)
    nb, tm = _seq_tiles(B, T)
    tok = lambda w: pl.BlockSpec((nb, tm, w), lambda b, i: (b, i, 0))
    mem = pl.BlockSpec((None, nb, MEM_TOKENS, D_MODEL), lambda b, i: (lm, b, 0, 0))
    sq = _lspec(l, (D_MODEL, D_MODEL))
    return pl.pallas_call(
        functools.partial(_post_kernel, nb=nb, tm=tm),
        grid=(B // nb, T // tm),
        in_specs=[_stream_spec(xs, nb, tm), tok(GDN_V), tok(SWA_Q), tok(SC_WIDTH), sq,
                  _lspec(l, (1, D_MODEL)), sq, _lspec(l, (1, MEM_HD)), mem, mem, sq],
        out_specs=_slab_spec(nb, tm),
        out_shape=jax.ShapeDtypeStruct((B, N_SLABS, T, LANES), F32),
        compiler_params=_params(("arbitrary", "arbitrary")),
        name="post",
    )(xs, oa, ob, oc, p['w_o'], p['norm_mem_g'], p['w_mq'], p['mem_q_norm_g'], mk, mv, p['w_mo'])


def _ffn_kernel(*refs, nb, tm, n_sub, has_state):
    h_ref, g_ref, wup_ref, wc_ref, wdn_ref = refs[:5]
    buf_ref = refs[5] if has_state else None
    o_ref, st_ref, carry_scr = refs[5 + int(has_state):]
    i = pl.program_id(1)
    ts = tm // n_sub
    L = ts // SUBLANES
    tail = 2 * SUBLANES

    @pl.when(i == 0)
    def _init():
        carry_scr[...] = jnp.zeros_like(carry_scr)
        if has_state:
            for b in range(nb):
                carry_scr[b, SUBLANES - 1:SUBLANES, :] = buf_ref[b, 0:1, :]
                carry_scr[b, tail - 1:tail, :] = buf_ref[b, 1:2, :]

    def conv_act(xn, lo, hi):
        cw = hi - lo
        last_sub = lax.broadcasted_iota(jnp.int32, (SUBLANES, cw), 0) == SUBLANES - 1
        conv = []
        for base in (0, D_FF):
            cols = slice(base + lo, base + hi)
            u = jnp.dot(xn, wup_ref[:, cols], preferred_element_type=F32)
            prev1, prev2 = [], []
            for b in range(nb):
                ub = u[b * ts:(b + 1) * ts]
                grp_a = jnp.where(last_sub, carry_scr[b, 0:SUBLANES, cols],
                                  ub[ts - tail:ts - SUBLANES])
                grp_b = jnp.where(last_sub, carry_scr[b, SUBLANES:tail, cols],
                                  ub[ts - SUBLANES:ts])
                fix_a = pltpu.roll(grp_a, 1, axis=0)
                fix_b = pltpu.roll(grp_b, 1, axis=0)
                prev1 += [fix_b, ub[0:ts - SUBLANES]]
                prev2 += [fix_a, fix_b] + ([ub[0:ts - tail]] if ts > tail else [])
                carry_scr[b, :, cols] = ub[ts - tail:ts]
            y = jnp.concatenate(prev2, axis=0) * wc_ref[0:1, cols]
            y = y + jnp.concatenate(prev1, axis=0) * wc_ref[1:2, cols]
            y = y + u * wc_ref[2:3, cols]
            conv.append(y)
        return (_silu(conv[0]) * conv[1]).astype(BF16)

    def down(acc, act, lo, hi):
        return acc + jnp.dot(act, wdn_ref[lo:hi, :], preferred_element_type=F32)

    n_ch = len(FFN_COL_CHUNKS)
    for sub in range(n_sub):
        t0 = sub * ts
        hs = jnp.concatenate(
            [jnp.concatenate([h_ref[b, c, pl.ds(t0 + j, SUBLANES, stride=L), :]
                              for b in range(nb) for j in range(L)], axis=0)
             for c in range(N_SLABS)], axis=-1)
        xn = _rms(hs, g_ref[...]).astype(BF16)
        acc = hs
        act = conv_act(xn, *FFN_COL_CHUNKS[0])
        for k in range(n_ch):
            nxt = conv_act(xn, *FFN_COL_CHUNKS[k + 1]) if k + 1 < n_ch else None
            acc = down(acc, act, *FFN_COL_CHUNKS[k])
            act = nxt
        for b in range(nb):
            for j in range(L):
                r = (b * L + j) * SUBLANES
                for c in range(N_SLABS):
                    o_ref[b, c, pl.ds(t0 + j, SUBLANES, stride=L), :] = acc[
                        r:r + SUBLANES, c * LANES:(c + 1) * LANES]

    @pl.when(i == pl.num_programs(1) - 1)
    def _fin():
        for b in range(nb):
            st_ref[b, 0:1, :] = carry_scr[b, SUBLANES - 1:SUBLANES, :]
            st_ref[b, 1:2, :] = carry_scr[b, tail - 1:tail, :]


def _ffn(h, l, p, state):
    B, _, T, _ = h.shape
    nb, tm = _seq_tiles(B, T)
    n_sub = 1
    if nb == 1 and T % (FFN_SUB_TILES * tm) == 0:
        n_sub, tm = FFN_SUB_TILES, FFN_SUB_TILES * tm
    assert (tm // n_sub) % (2 * SUBLANES) == 0
    tok = _slab_spec(nb, tm)
    in_specs = [tok, _lspec(l, (1, D_MODEL)), _lspec(l, (D_MODEL, 2 * D_FF), single_buffer=True),
                _lspec(l, (FFN_CONV, 2 * D_FF)), _lspec(l, (D_FF, D_MODEL), single_buffer=True)]
    args = [h, p['norm_ffn_g'], p['w_up'], p['w_ffn_conv'], p['w_down']]
    if state is not None:
        in_specs.append(pl.BlockSpec((None, nb, FFN_CONV - 1, 2 * D_FF), lambda b, i: (l, b, 0, 0)))
        args.append(state)
    return pl.pallas_call(
        functools.partial(_ffn_kernel, nb=nb, tm=tm, n_sub=n_sub, has_state=state is not None),
        grid=(B // nb, T // tm),
        in_specs=in_specs,
        out_specs=[tok, pl.BlockSpec((nb, FFN_CONV - 1, 2 * D_FF), lambda b, i: (b, 0, 0))],
        out_shape=[jax.ShapeDtypeStruct(h.shape, F32),
                   jax.ShapeDtypeStruct((B, FFN_CONV - 1, 2 * D_FF), F32)],
        scratch_shapes=[pltpu.VMEM((nb, 2 * SUBLANES, 2 * D_FF), F32)],
        compiler_params=_params(("arbitrary", "arbitrary")),
        name="ffn",
    )(*args)


def _unslab_kernel(x_ref, o_ref, *, nb):
    rows = _rows_from_slabs(x_ref, nb)
    tm = rows.shape[0] // nb
    for b in range(nb):
        o_ref[b] = rows[b * tm:(b + 1) * tm]


def _unslab(xs):
    B, T = _stream_dims(xs)
    nb, tm = _seq_tiles(B, T)
    return pl.pallas_call(
        functools.partial(_unslab_kernel, nb=nb),
        grid=(B // nb, T // tm),
        in_specs=[_slab_spec(nb, tm)],
        out_specs=pl.BlockSpec((nb, tm, D_MODEL), lambda b, i: (b, i, 0)),
        out_shape=jax.ShapeDtypeStruct((B, T, D_MODEL), F32),
        compiler_params=_params(("arbitrary", "arbitrary")),
        name="unslab",
    )(xs)


def _layer(x, l, p, mk, mv, lm, gdn_states, ffn_state, swa_cache, keep):
    B, T = _stream_dims(x)
    qkv, z, swa, sc, ab = _in_proj(x, l, p)
    oa, oc, s_new, sc_new = _gdn(qkv, ab, z, sc, l, p, gdn_states)
    if swa_cache is None:
        ob, kn = _swa_prompt(swa, l, p)
    else:
        ob, kn = _swa_sample(swa, swa_cache[0], swa_cache[1], l, p)
    h = _post(x, oa, ob, oc, mk, mv, lm, l, p)
    h, f_new = _ffn(h, l, p, ffn_state)
    gb_new = qkv[:, T - (GDN_CONV - 1):, :]
    sk = kn[:, T - keep:].reshape(B, keep, SWA_KV_HEADS, SWA_HD)
    sv = swa[:, T - keep:, SWA_Q + SWA_KV:].reshape(B, keep, SWA_KV_HEADS, SWA_HD)
    return h, s_new, gb_new, sc_new, f_new, sk, sv


def kernel(x_prompt, x_sample, mem_prompt, state_gdn, state_gdn_conv, cache_swa_k, cache_swa_v, state_sc_conv, cache_mem_k, cache_mem_v, state_ffn_conv, norm_mix_g, w_in, w_gdn_conv, gdn_a_log, gdn_dt_bias, gdn_norm_g, swa_q_norm_g, swa_k_norm_g, swa_sinks, w_sc_conv, w_o, norm_mem_g, mem_in_norm_g, w_mq, w_mk, w_mv, mem_q_norm_g, mem_k_norm_g, w_mo, norm_ffn_g, w_up, w_ffn_conv, w_down):
    depth = w_in.shape[0]
    Bp, Tp, _ = x_prompt.shape
    Bs, Ts, _ = x_sample.shape
    keep = min(WINDOW, Tp)
    past = cache_swa_k.shape[2]

    qz = GDN_CONV_CH + GDN_V
    w_in_b = w_in.astype(BF16)
    w_in_r = jnp.concatenate(
        [w_in_b[:, :, :qz], w_in_b[:, :, qz + 2 * GDN_HEADS:], w_in_b[:, :, qz:qz + 2 * GDN_HEADS],
         jnp.zeros((depth, D_MODEL, AB_W - 2 * GDN_HEADS), BF16)], axis=-1)
    hp = jnp.zeros((depth, SUBLANES, LANES), F32)
    hp = hp.at[:, 0, :GDN_HEADS].set(gdn_a_log).at[:, 1, :GDN_HEADS].set(gdn_dt_bias)
    sinks = jnp.zeros((depth, 1, LANES), F32).at[:, 0, :SWA_HEADS].set(swa_sinks)
    row = lambda a: a[:, None, :]
    p = {
        'norm_mix_g': row(norm_mix_g), 'w_in': w_in_r, 'w_gdn_conv': w_gdn_conv, 'hp': hp,
        'gdn_norm_g': row(gdn_norm_g), 'swa_q_norm_g': row(swa_q_norm_g),
        'swa_k_norm_g': row(swa_k_norm_g), 'swa_sinks': sinks,
        'swa_q_norm_g_tiled': row(jnp.tile(swa_q_norm_g, (1, SWA_HEADS))),
        'swa_k_norm_g_tiled': row(jnp.tile(swa_k_norm_g, (1, SWA_KV_HEADS))), 'w_sc_conv': w_sc_conv,
        'w_o': w_o.astype(BF16), 'norm_mem_g': row(norm_mem_g), 'mem_in_norm_g': row(mem_in_norm_g),
        'w_mq': w_mq.astype(BF16), 'w_mk': w_mk.astype(BF16), 'w_mv': w_mv.astype(BF16),
        'mem_q_norm_g': row(mem_q_norm_g), 'mem_k_norm_g': row(mem_k_norm_g),
        'w_mo': w_mo.astype(BF16), 'norm_ffn_g': row(norm_ffn_g), 'w_up': w_up.astype(BF16),
        'w_ffn_conv': w_ffn_conv, 'w_down': w_down.astype(BF16),
    }
    cache_k = cache_swa_k.reshape(depth, Bs, past, SWA_KV)
    cache_v = cache_swa_v.reshape(depth, Bs, past, SWA_KV)
    mem_k_s = cache_mem_k.reshape(depth, Bs, MEM_TOKENS, D_MODEL)
    mem_v_s = cache_mem_v.reshape(depth, Bs, MEM_TOKENS, D_MODEL)

    hp_, hs_ = x_prompt, x_sample
    outs_p = {k: [] for k in ('S', 'gb', 'sk', 'sv', 'sb', 'mk', 'mv', 'fb')}
    outs_s = {k: [] for k in ('S', 'gb', 'sk', 'sv', 'sb', 'fb')}
    for l in range(depth):
        mk, mv = _mem_kv(mem_prompt, l, p)
        hp_, S, gb, sb, fb, sk, sv = _layer(hp_, l, p, mk[None], mv[None], 0, None, None, None, keep)
        outs_p['S'].append(S); outs_p['gb'].append(gb); outs_p['sb'].append(sb); outs_p['fb'].append(fb)
        outs_p['sk'].append(sk); outs_p['sv'].append(sv)
        outs_p['mk'].append(mk.reshape(Bp, MEM_TOKENS, MEM_HEADS, MEM_HD))
        outs_p['mv'].append(mv.reshape(Bp, MEM_TOKENS, MEM_HEADS, MEM_HD))
        hs_, S, gb, sb, fb, sk, sv = _layer(
            hs_, l, p, mem_k_s, mem_v_s, l, (state_gdn, state_gdn_conv, state_sc_conv),
            state_ffn_conv, (cache_k, cache_v), Ts)
        outs_s['S'].append(S); outs_s['gb'].append(gb); outs_s['sb'].append(sb); outs_s['fb'].append(fb)
        outs_s['sk'].append(sk); outs_s['sv'].append(sv)
    st = jnp.stack
    return (_unslab(hp_), _unslab(hs_),
            st(outs_p['S']), st(outs_p['gb']), st(outs_p['sk']), st(outs_p['sv']), st(outs_p['sb']),
            st(outs_p['mk']), st(outs_p['mv']), st(outs_p['fb']),
            st(outs_s['S']), st(outs_s['gb']), st(outs_s['sk']), st(outs_s['sv']), st(outs_s['sb']),
            st(outs_s['fb']))
```

```python
import functools

import jax
import jax.numpy as jnp
from jax import lax
from jax.experimental import pallas as pl
from jax.experimental.pallas import tpu as pltpu

F32 = jnp.float32
BF16 = jnp.bfloat16

D_MODEL = 1024
CHUNK = 64
EPS = 1e-6
NEG_INF = -1e30
GDN_HEADS = 4
GDN_DK = 128
GDN_DV = 128
GDN_CONV = 4
GDN_QK = GDN_HEADS * GDN_DK
GDN_V = GDN_HEADS * GDN_DV
GDN_CONV_CH = 2 * GDN_QK + GDN_V
SWA_HEADS = 4
SWA_KV_HEADS = 2
SWA_GROUP = SWA_HEADS // SWA_KV_HEADS
SWA_HD = 64
WINDOW = 128
WIN_CHUNKS = WINDOW // CHUNK
SWA_Q = SWA_HEADS * SWA_HD
SWA_KV = SWA_KV_HEADS * SWA_HD
SWA_W = SWA_Q + 2 * SWA_KV
assert SWA_GROUP == SWA_KV_HEADS == 2 and SWA_KV == 128
SC_WIDTH = 256
SC_CONV = 3
MEM_TOKENS = 256
MEM_HEADS = 4
MEM_HD = D_MODEL // MEM_HEADS
D_FF = 2816
FFN_CONV = 3
LANES = 128
SUBLANES = 8
AB_W = LANES
SC_W = 3 * SC_WIDTH
IN_COLS = (
    (0, GDN_CONV_CH),
    (GDN_CONV_CH, GDN_CONV_CH + GDN_V),
    (GDN_CONV_CH + GDN_V, GDN_CONV_CH + GDN_V + SWA_W),
    (GDN_CONV_CH + GDN_V + SWA_W, GDN_CONV_CH + GDN_V + SWA_W + SC_W),
    (GDN_CONV_CH + GDN_V + SWA_W + SC_W, GDN_CONV_CH + GDN_V + SWA_W + SC_W + AB_W),
)
D_IN_PAD = IN_COLS[-1][1]
FFN_COL_CHUNKS = ((0, 1024), (1024, 2048), (2048, D_FF))
GDN_CHUNKS_PER_STEP = 4
SUB_TILES = 2
VMEM_LIMIT = 56 * 1024 * 1024
TOKEN_TILE = 512
SWA_QUERY_BLOCK = 512
MAX_ROWS_SMALL_SEQ = 128


def _seq_tiles(B, T):
    if T >= TOKEN_TILE:
        return 1, TOKEN_TILE
    nb = max(1, min(B, MAX_ROWS_SMALL_SEQ // T))
    while B % nb:
        nb -= 1
    return nb, T


def _sub_tiles(B, T):
    nb, tm = _seq_tiles(B, T)
    if nb == 1 and T % (SUB_TILES * tm) == 0:
        return nb, SUB_TILES * tm, SUB_TILES
    return nb, tm, 1


def _dot(a, b):
    return jnp.dot(a.astype(BF16), b.astype(BF16), preferred_element_type=F32)


def _dot_nt(a, b):
    return lax.dot_general(a.astype(BF16), b.astype(BF16), (((1,), (1,)), ((), ())),
                           preferred_element_type=F32)


def _dot_tn(a, b):
    return lax.dot_general(a.astype(BF16), b.astype(BF16), (((0,), (0,)), ((), ())),
                           preferred_element_type=F32)


def _rms(x, g):
    return x * lax.rsqrt(jnp.mean(x * x, axis=-1, keepdims=True) + EPS) * g


def _silu(x):
    return x * jax.nn.sigmoid(x)


def _lspec(l, shape, single_buffer=False):
    nd = len(shape)
    mode = dict(pipeline_mode=pl.Buffered(1)) if single_buffer else {}
    return pl.BlockSpec((None,) + tuple(shape), lambda *_: (l,) + (0,) * nd, **mode)


def _params(sem):
    return pltpu.CompilerParams(dimension_semantics=sem, vmem_limit_bytes=VMEM_LIMIT)


def _tok_spec(nb, tm, w):
    return pl.BlockSpec((nb, tm, w), lambda b, i: (b, i, 0))


def _block_rows(ref, nb, t0=0, ts=None):
    ts = ref.shape[-2] - t0 if ts is None else ts
    if len(ref.shape) == 3:
        seqs = [ref[b, t0:t0 + ts, :] for b in range(nb)]
    else:
        seqs = [jnp.concatenate([ref[b, c, t0:t0 + ts, :] for c in range(ref.shape[1])], axis=-1)
                for b in range(nb)]
    return seqs[0] if nb == 1 else jnp.concatenate(seqs, axis=0)


def _in_proj_kernel(x_ref, g_ref, w_ref, qkv_ref, z_ref, swa_ref, sc_ref, ab_ref, *, nb, tm):
    xn = _rms(_block_rows(x_ref, nb), g_ref[...]).astype(BF16)
    for o_ref, (lo, hi) in zip((qkv_ref, z_ref, swa_ref, sc_ref, ab_ref), IN_COLS):
        res = jnp.dot(xn, w_ref[:, lo:hi], preferred_element_type=F32)
        for b in range(nb):
            rows = res[b * tm:(b + 1) * tm]
            if len(o_ref.shape) == 4:
                for c in range(o_ref.shape[1]):
                    o_ref[b, c] = rows[:, c * LANES:(c + 1) * LANES]
            else:
                o_ref[b] = rows


def _in_proj(xs, l, p):
    B, T, _ = xs.shape
    nb, tm = _seq_tiles(B, T)
    widths = [hi - lo for lo, hi in IN_COLS]
    slabbed = (True, True, False, False, False)
    out_specs, out_shape = [], []
    for wd, sl in zip(widths, slabbed):
        if sl:
            out_specs.append(pl.BlockSpec((nb, wd // LANES, tm, LANES), lambda b, i: (b, 0, i, 0)))
            out_shape.append(jax.ShapeDtypeStruct((B, wd // LANES, T, LANES), F32))
        else:
            out_specs.append(pl.BlockSpec((nb, tm, wd), lambda b, i: (b, i, 0)))
            out_shape.append(jax.ShapeDtypeStruct((B, T, wd), F32))
    return pl.pallas_call(
        functools.partial(_in_proj_kernel, nb=nb, tm=tm),
        grid=(B // nb, T // tm),
        in_specs=[_tok_spec(nb, tm, D_MODEL), _lspec(l, (1, D_MODEL)),
                  _lspec(l, (D_MODEL, D_IN_PAD))],
        out_specs=out_specs,
        out_shape=out_shape,
        compiler_params=_params(("arbitrary", "arbitrary")),
        name="in_proj",
    )(xs, p['norm_mix_g'], p['w_in'])


def _cumsum_time(groups):
    prefix = [groups[0]]
    for g in groups[1:]:
        prefix.append(prefix[-1] + g)
    sub = lax.broadcasted_iota(jnp.int32, groups[0].shape, 0)
    run = prefix[-1]
    s = 1
    while s < SUBLANES:
        run = run + jnp.where(sub >= s, pltpu.roll(run, s, axis=0), 0.0)
        s *= 2
    before = jnp.where(sub >= 1, pltpu.roll(run, 1, axis=0), 0.0)
    return [p + before for p in prefix]


def _gdn_kernel(*refs, bb, C, nc, has_state):
    qkv_ref, ab_ref, z_ref, sc_ref, wconv_ref, wsc_ref, hp_ref, gn_ref = refs[:8]
    state_refs = refs[8:11] if has_state else ()
    oa_ref, oc_ref, sout_ref, scout_ref, s_scr, xp_scr, scp_scr = refs[8 + len(state_refs):]
    n = pl.program_id(1)
    pad = SUBLANES
    L = C // SUBLANES
    n_slabs = GDN_CONV_CH // LANES

    def time_of(r):
        return (r % SUBLANES) * L + r // SUBLANES

    def token_row(t):
        return (t % L) * SUBLANES + t // L

    @pl.when(n == 0)
    def _init():
        xp_scr[...] = jnp.zeros_like(xp_scr)
        scp_scr[...] = jnp.zeros_like(scp_scr)
        if has_state:
            s0_ref, gbuf_ref, scbuf_ref = state_refs
            s_scr[...] = s0_ref[...]
            for r in range(GDN_CONV - 1):
                row = token_row(C - (GDN_CONV - 1) + r)
                for b in range(bb):
                    for m in range(n_slabs):
                        xp_scr[b, m, row:row + 1, :] = gbuf_ref[b, r:r + 1, m * LANES:(m + 1) * LANES]
            scp_scr[:, pad - (SC_CONV - 1):pad, :] = scbuf_ref[...]
        else:
            s_scr[...] = jnp.zeros_like(s_scr)

    t_row = time_of(lax.broadcasted_iota(jnp.int32, (C, C), 0))
    t_col = time_of(lax.broadcasted_iota(jnp.int32, (C, C), 1))
    incl = t_row >= t_col
    strict = t_row > t_col
    sub = lax.broadcasted_iota(jnp.int32, (SUBLANES, LANES), 0)
    neg_a = -jnp.exp(hp_ref[0:1, :])
    dt_b = hp_ref[1:2, :]
    n_sq = C.bit_length() - 2

    R = nc * C

    def chunk_groups(window):
        return [window(j) for j in range(L)]

    def shifted(cur, prev, k, cache):
        out = []
        for j in range(L):
            d = j - k
            if d >= 0:
                out.append(cur[d])
                continue
            m = (-d + L - 1) // L
            src = d + m * L
            if (src, m) not in cache:
                cache[src, m] = pltpu.roll(
                    jnp.where(sub >= SUBLANES - m, prev[src], cur[src]), m, axis=0)
            out.append(cache[src, m])
        return jnp.concatenate(out, axis=0)

    ys, gcs, gcts, betas = {}, {}, {}, {}
    for b in range(bb):
        for m in range(n_slabs):
            lanes = slice(m * LANES, (m + 1) * LANES)
            prev = [xp_scr[b, m, j * SUBLANES:(j + 1) * SUBLANES, :] for j in range(L)]
            for c in range(nc):
                cur = chunk_groups(
                    lambda j: qkv_ref[b, m, pl.ds(c * C + j, SUBLANES, stride=L), :])
                cache = {}
                y = shifted(cur, prev, 3, cache) * wconv_ref[0:1, lanes]
                y = y + shifted(cur, prev, 2, cache) * wconv_ref[1:2, lanes]
                y = y + shifted(cur, prev, 1, cache) * wconv_ref[2:3, lanes]
                y = y + jnp.concatenate(cur, axis=0) * wconv_ref[3:4, lanes]
                ys[b, c, m] = _silu(y)
                prev = cur
            xp_scr[b, m] = jnp.concatenate(prev, axis=0)

        for c in range(nc):
            ab = jnp.concatenate(chunk_groups(
                lambda j: ab_ref[b, pl.ds(c * C + j, SUBLANES, stride=L), :]), axis=0)
            t = ab + dt_b
            g = neg_a * (jnp.maximum(t, 0.0) + jnp.log1p(jnp.exp(-jnp.abs(t))))
            betas[b, c] = jax.nn.sigmoid(ab)
            gc = jnp.concatenate(_cumsum_time(
                [g[j * SUBLANES:(j + 1) * SUBLANES] for j in range(L)]), axis=0)
            gcs[b, c] = gc
            gcts[b, c] = jnp.transpose(jnp.concatenate(
                [gc, jnp.zeros((LANES - C, LANES), F32)], axis=0))

        scv = sc_ref[b]
        prod = scv[:, SC_WIDTH:2 * SC_WIDTH] * scv[:, 2 * SC_WIDTH:3 * SC_WIDTH]
        pext = jnp.concatenate([scp_scr[b], prod], axis=0)
        cu = pltpu.roll(pext, 2, axis=0)[pad:] * wsc_ref[0:1, :]
        cu = cu + pltpu.roll(pext, 1, axis=0)[pad:] * wsc_ref[1:2, :]
        cu = cu + prod * wsc_ref[2:3, :]
        scp_scr[b] = prod[R - pad:R, :]
        oc_ref[b] = scv[:, 0:SC_WIDTH] * cu

    problems = [(c, b, h) for c in range(nc) for b in range(bb) for h in range(GDN_HEADS)]
    every = range(len(problems))
    qs, ks, vs, kbs, bts, egs, e_incls, e_stricts, kds, gls = ([] for _ in range(10))
    for c, b, h in problems:
        gc = gcs[b, c]
        q = ys[b, c, h]
        k = ys[b, c, GDN_HEADS + h]
        q = q * lax.rsqrt(jnp.sum(q * q, axis=-1, keepdims=True) + EPS) * (GDN_DK ** -0.5)
        k = k * lax.rsqrt(jnp.sum(k * k, axis=-1, keepdims=True) + EPS)
        gci = gc[:, h:h + 1]
        gcj = gcts[b, c][h:h + 1, 0:C]
        g_last = gc[C - 1:C, h:h + 1]
        e_incl = jnp.exp(jnp.where(incl, gci - gcj, NEG_INF))
        qs.append(q)
        ks.append(k)
        kbs.append(k.astype(BF16))
        vs.append(ys[b, c, 2 * GDN_HEADS + h])
        bts.append(betas[b, c][:, GDN_HEADS + h:GDN_HEADS + h + 1])
        egs.append(jnp.exp(gci))
        e_incls.append(e_incl)
        e_stricts.append(jnp.where(strict, e_incl, 0.0))
        kds.append(k * jnp.exp(g_last - gci))
        gls.append(jnp.exp(g_last))

    kks = [_dot_nt(kbs[i], kbs[i]) for i in every]
    qks = [_dot_nt(qs[i], kbs[i]) * e_incls[i] for i in every]
    p_pows = [-(bts[i] * kks[i] * e_stricts[i]) for i in every]
    n_accs = list(p_pows)
    for _ in range(n_sq):
        pbs = [p.astype(BF16) for p in p_pows]
        p_pows = [_dot(pb, pb) for pb in pbs]
        n_accs = [n_accs[i] + p_pows[i] + _dot(p_pows[i], n_accs[i]) for i in every]
    rhss = [jnp.concatenate([bts[i] * vs[i], (bts[i] * egs[i]) * ks[i]], axis=-1) for i in every]
    sols = [rhss[i] + _dot(n_accs[i], rhss[i]) for i in every]

    pairs = [(b, h) for b in range(bb) for h in range(GDN_HEADS)]
    here = range(len(pairs))
    states = [s_scr[b, h] for b, h in pairs]
    for c in range(nc):
        base = c * len(pairs)
        sbs = [s.astype(BF16) for s in states]
        o_inter = [_dot(qs[base + j] * egs[base + j], sbs[j]) for j in here]
        us = [sols[base + j][:, :GDN_DV] - _dot(sols[base + j][:, GDN_DV:], sbs[j]) for j in here]
        outs = [o_inter[j] + _dot(qks[base + j], us[j]) for j in here]
        states = [gls[base + j] * states[j] + _dot_tn(kds[base + j], us[j]) for j in here]
        for j, (b, h) in enumerate(pairs):
            zh = jnp.concatenate(chunk_groups(
                lambda i: z_ref[b, h, pl.ds(c * C + i, SUBLANES, stride=L), :]), axis=0)
            res = _rms(outs[j], gn_ref[...]) * _silu(zh)
            for i in range(L):
                oa_ref[b, h, pl.ds(c * C + i, SUBLANES, stride=L), :] = res[
                    i * SUBLANES:(i + 1) * SUBLANES]
    for j, (b, h) in enumerate(pairs):
        s_scr[b, h] = states[j]

    @pl.when(n == pl.num_programs(1) - 1)
    def _fin():
        sout_ref[...] = s_scr[...]
        scout_ref[...] = scp_scr[:, pad - (SC_CONV - 1):pad, :]


def _gdn(qkv, ab, z, sc, l, p, states):
    B, n_slabs, T, _ = qkv.shape
    C = CHUNK if T % CHUNK == 0 else T
    assert C % SUBLANES == 0
    bb = 2 if B % 2 == 0 else 1
    nc = GDN_CHUNKS_PER_STEP if (T // C) % GDN_CHUNKS_PER_STEP == 0 else 1
    tok = lambda w: pl.BlockSpec((bb, nc * C, w), lambda i, n: (i, n, 0))
    slabs = lambda k: pl.BlockSpec((bb, k, nc * C, LANES), lambda i, n: (i, 0, n, 0))
    per_b = lambda *s: pl.BlockSpec((bb,) + s, lambda i, n: (i,) + (0,) * len(s))
    per_lb = lambda *s: pl.BlockSpec((None, bb) + s, lambda i, n: (l, i) + (0,) * len(s))
    in_specs = [slabs(n_slabs), tok(AB_W), slabs(GDN_HEADS), tok(SC_W),
                _lspec(l, (GDN_CONV, GDN_CONV_CH)), _lspec(l, (SC_CONV, SC_WIDTH)),
                _lspec(l, (SUBLANES, LANES)), _lspec(l, (1, GDN_DV))]
    args = [qkv, ab, z, sc, p['w_gdn_conv'], p['w_sc_conv'], p['hp'], p['gdn_norm_g']]
    if states is not None:
        in_specs += [per_lb(GDN_HEADS, GDN_DK, GDN_DV), per_lb(GDN_CONV - 1, GDN_CONV_CH),
                     per_lb(SC_CONV - 1, SC_WIDTH)]
        args += list(states)
    return pl.pallas_call(
        functools.partial(_gdn_kernel, bb=bb, C=C, nc=nc, has_state=states is not None),
        grid=(B // bb, T // (nc * C)),
        in_specs=in_specs,
        out_specs=[slabs(GDN_HEADS), tok(SC_WIDTH),
                   per_b(GDN_HEADS, GDN_DK, GDN_DV), per_b(SC_CONV - 1, SC_WIDTH)],
        out_shape=[jax.ShapeDtypeStruct((B, GDN_HEADS, T, GDN_DV), F32),
                   jax.ShapeDtypeStruct((B, T, SC_WIDTH), F32),
                   jax.ShapeDtypeStruct((B, GDN_HEADS, GDN_DK, GDN_DV), F32),
                   jax.ShapeDtypeStruct((B, SC_CONV - 1, SC_WIDTH), F32)],
        scratch_shapes=[pltpu.VMEM((bb, GDN_HEADS, GDN_DK, GDN_DV), F32),
                        pltpu.VMEM((bb, n_slabs, C, LANES), F32),
                        pltpu.VMEM((bb, SUBLANES, SC_WIDTH), F32)],
        compiler_params=_params(("arbitrary", "arbitrary")),
        name="gdn",
    )(*args)


def _swa_norm_k(x, gk):
    return jnp.concatenate(
        [_rms(x[:, SWA_Q + i * SWA_HD:SWA_Q + (i + 1) * SWA_HD], gk) for i in range(SWA_KV_HEADS)],
        axis=-1)


def _swa_norm_q(x, gq):
    return [_rms(x[:, h * SWA_HD:(h + 1) * SWA_HD], gq) for h in range(SWA_HEADS)]


def _group_queries(qn, g, r0, r1, sinks):
    heads = range(g * SWA_GROUP, (g + 1) * SWA_GROUP)
    q = jnp.concatenate([qn[h][r0:r1] for h in heads], axis=0)
    sink = jnp.concatenate(
        [jnp.broadcast_to(sinks[:, h:h + 1], (r1 - r0, 1)) for h in heads], axis=0)
    return q, sink


def _attend_many(problems):
    scores = [_dot_nt(q, k) * (SWA_HD ** -0.5) for q, k, _, _, _ in problems]
    probs = []
    for s, (_, _, _, valid, sink) in zip(scores, problems):
        if valid is not None:
            s = jnp.where(valid, s, NEG_INF)
        m = jnp.maximum(jnp.max(s, axis=-1, keepdims=True), sink)
        p = jnp.exp(s - m)
        probs.append(p / (jnp.sum(p, axis=-1, keepdims=True) + jnp.exp(sink - m)))
    return [_dot(p, pr[2]) for p, pr in zip(probs, problems)]


def _ungroup(outs, rows):
    return jnp.concatenate(
        [o[j * rows:(j + 1) * rows] for o in outs for j in range(SWA_GROUP)], axis=-1)


def _split_bf16(x):
    hi = x.astype(BF16)
    return hi, (x - hi.astype(F32)).astype(BF16)


def _sum_dot(x, ones):
    hi, lo = _split_bf16(x)
    return (jnp.dot(hi, ones, preferred_element_type=F32)
            + jnp.dot(lo, ones, preferred_element_type=F32))


def _head_rms(x, g_tiled):
    w = x.shape[1]
    same_head = (lax.broadcasted_iota(jnp.int32, (w, w), 0) // SWA_HD
                 == lax.broadcasted_iota(jnp.int32, (w, w), 1) // SWA_HD)
    ss = _sum_dot(x * x, jnp.where(same_head, 1.0, 0.0).astype(BF16))
    return x * lax.rsqrt(ss * (1.0 / SWA_HD) + EPS) * g_tiled


def _swa_prompt_kernel(x_ref, halo_ref, gq_ref, gk_ref, sink_ref, o_ref, kn_ref, *, qb):
    i = pl.program_id(1)
    band = (WIN_CHUNKS + 1) * CHUNK
    n_chunks = qb // CHUNK
    x = x_ref[...]
    hx = halo_ref[...]
    kcols = slice(SWA_Q, SWA_Q + SWA_KV)
    vcols = slice(SWA_Q + SWA_KV, SWA_W)
    qn = _head_rms(x[:, 0:SWA_Q], gq_ref[...])
    k_all = _head_rms(jnp.concatenate([hx[:, kcols], x[:, kcols]], axis=0), gk_ref[...])
    v_all = jnp.concatenate([hx[:, vcols], x[:, vcols]], axis=0)
    kn_ref[...] = k_all[WINDOW:]

    lane_half = lax.broadcasted_iota(jnp.int32, (1, SWA_KV), 1) // SWA_HD

    def variants(a):
        rot = pltpu.roll(a, SWA_HD, axis=1)
        return [[jnp.where(lane_half == j, a if j == g else rot, 0.0).astype(BF16)
                 for j in range(SWA_GROUP)] for g in range(SWA_KV_HEADS)]

    k_var, v_var = variants(k_all), variants(v_all)
    ones = jnp.ones((band, SWA_KV), BF16)
    local_chunk = lax.broadcasted_iota(jnp.int32, (1, band), 1) // CHUNK
    problems = [(c, g, j) for c in range(n_chunks) for g in range(SWA_KV_HEADS)
                for j in range(SWA_GROUP)]
    scores = []
    for c, g, j in problems:
        q = qn[c * CHUNK:(c + 1) * CHUNK, g * SWA_KV:(g + 1) * SWA_KV]
        scores.append(_dot_nt(q, k_var[g][j][c * CHUNK:c * CHUNK + band]) * (SWA_HD ** -0.5))
    ps, tails = [], []
    for s, (c, g, j) in zip(scores, problems):
        valid = local_chunk + (i * n_chunks + c - WIN_CHUNKS) >= 0
        s = jnp.where(valid, s, NEG_INF)
        sink = sink_ref[:, g * SWA_GROUP + j:g * SWA_GROUP + j + 1]
        m = jnp.maximum(jnp.max(s, axis=-1, keepdims=True), sink)
        ps.append(jnp.exp(s - m))
        tails.append(jnp.exp(sink - m))
    outs = []
    for p, tail, (c, g, j) in zip(ps, tails, problems):
        denom = _sum_dot(p, ones) + tail
        outs.append(_dot(p, v_var[g][j][c * CHUNK:c * CHUNK + band]) / denom)
    for c in range(n_chunks):
        for g in range(SWA_KV_HEADS):
            k0 = (c * SWA_KV_HEADS + g) * SWA_GROUP
            o_ref[c * CHUNK:(c + 1) * CHUNK, g * SWA_KV:(g + 1) * SWA_KV] = sum(
                outs[k0 + 1:k0 + SWA_GROUP], outs[k0])


def _swa_prompt(swa, l, p):
    B, T, _ = swa.shape
    qb = min(SWA_QUERY_BLOCK, T)
    assert T % qb == 0 and qb % WINDOW == 0
    halo_per_block = qb // WINDOW
    return pl.pallas_call(
        functools.partial(_swa_prompt_kernel, qb=qb),
        grid=(B, T // qb),
        in_specs=[pl.BlockSpec((None, qb, SWA_W), lambda b, i: (b, i, 0)),
                  pl.BlockSpec((None, WINDOW, SWA_W),
                               lambda b, i: (b, jnp.maximum(i * halo_per_block - 1, 0), 0)),
                  _lspec(l, (1, SWA_Q)), _lspec(l, (1, SWA_KV)), _lspec(l, (1, LANES))],
        out_specs=[pl.BlockSpec((None, qb, SWA_Q), lambda b, i: (b, i, 0)),
                   pl.BlockSpec((None, qb, SWA_KV), lambda b, i: (b, i, 0))],
        out_shape=[jax.ShapeDtypeStruct((B, T, SWA_Q), F32),
                   jax.ShapeDtypeStruct((B, T, SWA_KV), F32)],
        compiler_params=_params(("arbitrary", "arbitrary")),
        name="swa_prompt",
    )(swa, swa, p['swa_q_norm_g_tiled'], p['swa_k_norm_g_tiled'], p['swa_sinks'])


def _swa_sample_kernel(x_ref, kp_ref, vp_ref, gq_ref, gk_ref, sink_ref, o_ref, kn_ref, *, bb, T):
    problems = []
    for b in range(bb):
        x = x_ref[b]
        kn = _swa_norm_k(x, gk_ref[...])
        kn_ref[b] = kn
        k_all = jnp.concatenate([kp_ref[b], kn], axis=0)
        v_all = jnp.concatenate([vp_ref[b], x[:, SWA_Q + SWA_KV:SWA_W]], axis=0)
        qn = _swa_norm_q(x, gq_ref[...])
        for g in range(SWA_KV_HEADS):
            q, sink = _group_queries(qn, g, 0, T, sink_ref[...])
            cols = slice(g * SWA_HD, (g + 1) * SWA_HD)
            problems.append((q, k_all[:, cols], v_all[:, cols], None, sink))
    outs = _attend_many(problems)
    for b in range(bb):
        o_ref[b] = _ungroup(outs[b * SWA_KV_HEADS:(b + 1) * SWA_KV_HEADS], T)


def _swa_sample(swa, k_past, v_past, l, p):
    B, T, _ = swa.shape
    past = k_past.shape[2]
    bb = 4 if B % 4 == 0 else 1
    blk = lambda r, w: pl.BlockSpec((bb, r, w), lambda b: (b, 0, 0))
    cache = pl.BlockSpec((None, bb, past, SWA_KV), lambda b: (l, b, 0, 0))
    return pl.pallas_call(
        functools.partial(_swa_sample_kernel, bb=bb, T=T),
        grid=(B // bb,),
        in_specs=[blk(T, SWA_W), cache, cache,
                  _lspec(l, (1, SWA_HD)), _lspec(l, (1, SWA_HD)), _lspec(l, (1, LANES))],
        out_specs=[blk(T, SWA_Q), blk(T, SWA_KV)],
        out_shape=[jax.ShapeDtypeStruct((B, T, SWA_Q), F32),
                   jax.ShapeDtypeStruct((B, T, SWA_KV), F32)],
        compiler_params=_params(("arbitrary",)),
        name="swa_sample",
    )(swa, k_past, v_past, p['swa_q_norm_g'], p['swa_k_norm_g'], p['swa_sinks'])


def _mem_kv_kernel(m_ref, g_ref, wk_ref, wv_ref, gk_ref, k_ref, v_ref):
    m = _rms(m_ref[...], g_ref[...]).astype(BF16)
    k = jnp.dot(m, wk_ref[...], preferred_element_type=F32)
    k_ref[...] = jnp.concatenate(
        [_rms(k[:, h * MEM_HD:(h + 1) * MEM_HD], gk_ref[...]) for h in range(MEM_HEADS)], axis=-1)
    v_ref[...] = jnp.dot(m, wv_ref[...], preferred_element_type=F32)


def _mem_kv(mem, l, p):
    B, M, _ = mem.shape
    blk = pl.BlockSpec((None, M, D_MODEL), lambda b: (b, 0, 0))
    return pl.pallas_call(
        _mem_kv_kernel,
        grid=(B,),
        in_specs=[blk, _lspec(l, (1, D_MODEL)), _lspec(l, (D_MODEL, D_MODEL)),
                  _lspec(l, (D_MODEL, D_MODEL)), _lspec(l, (1, MEM_HD))],
        out_specs=[blk, blk],
        out_shape=[jax.ShapeDtypeStruct((B, M, D_MODEL), F32)] * 2,
        compiler_params=_params(("arbitrary",)),
        name="mem_kv",
    )(mem, p['mem_in_norm_g'], p['w_mk'], p['w_mv'], p['mem_k_norm_g'])


def _post_kernel(x_ref, a_ref, b_ref, c_ref, wo_ref, gm_ref, wq_ref, gq_ref, mk_ref, mv_ref,
                 wmo_ref, o_ref, *, nb, tm, n_sub):
    ts = tm // n_sub
    probs = [(b, slice(hd * MEM_HD, (hd + 1) * MEM_HD)) for b in range(nb) for hd in range(MEM_HEADS)]
    hs, scores = [], []
    for sub in range(n_sub):
        rows = lambda ref: _block_rows(ref, nb, sub * ts, ts)
        h = rows(x_ref)
        h = h + (jnp.dot(rows(a_ref).astype(BF16), wo_ref[0:GDN_V, :], preferred_element_type=F32)
                 + jnp.dot(rows(b_ref).astype(BF16), wo_ref[GDN_V:GDN_V + SWA_Q, :],
                           preferred_element_type=F32)
                 + jnp.dot(rows(c_ref).astype(BF16), wo_ref[GDN_V + SWA_Q:, :],
                           preferred_element_type=F32))
        hs.append(h)
        hn = _rms(h, gm_ref[...]).astype(BF16)
        q = jnp.dot(hn, wq_ref[...], preferred_element_type=F32)
        qhs = [_rms(q[b * ts:(b + 1) * ts, sl], gq_ref[...]) for b, sl in probs]
        scores.append([_dot_nt(qh, mk_ref[b, :, sl]) * (MEM_HD ** -0.5)
                       for qh, (b, sl) in zip(qhs, probs)])
    ps = []
    for sub in range(n_sub):
        ps.append([])
        for s in scores[sub]:
            p = jnp.exp(s - jnp.max(s, axis=-1, keepdims=True))
            ps[sub].append(p / jnp.sum(p, axis=-1, keepdims=True))
    for sub in range(n_sub):
        outs = [_dot(p, mv_ref[b, :, sl]) for p, (b, sl) in zip(ps[sub], probs)]
        o = jnp.concatenate(
            [jnp.concatenate(outs[b * MEM_HEADS:(b + 1) * MEM_HEADS], axis=-1) for b in range(nb)],
            axis=0).astype(BF16)
        res = hs[sub] + jnp.dot(o, wmo_ref[...], preferred_element_type=F32)
        for b in range(nb):
            o_ref[b, sub * ts:(sub + 1) * ts, :] = res[b * ts:(b + 1) * ts]


def _post(xs, oa, ob, oc, mk, mv, lm, l, p):
    B, T, _ = xs.shape
    nb, tm, n_sub = _sub_tiles(B, T)
    tok = lambda w: _tok_spec(nb, tm, w)
    mem = pl.BlockSpec((None, nb, MEM_TOKENS, D_MODEL), lambda b, i: (lm, b, 0, 0))
    sq = _lspec(l, (D_MODEL, D_MODEL), single_buffer=True)
    return pl.pallas_call(
        functools.partial(_post_kernel, nb=nb, tm=tm, n_sub=n_sub),
        grid=(B // nb, T // tm),
        in_specs=[tok(D_MODEL),
                  pl.BlockSpec((nb, GDN_HEADS, tm, GDN_DV), lambda b, i: (b, 0, i, 0)),
                  tok(SWA_Q), tok(SC_WIDTH), sq,
                  _lspec(l, (1, D_MODEL)), sq, _lspec(l, (1, MEM_HD)), mem, mem, sq],
        out_specs=tok(D_MODEL),
        out_shape=jax.ShapeDtypeStruct((B, T, D_MODEL), F32),
        compiler_params=_params(("arbitrary", "arbitrary")),
        name="post",
    )(xs, oa, ob, oc, p['w_o'], p['norm_mem_g'], p['w_mq'], p['mem_q_norm_g'], mk, mv, p['w_mo'])


def _ffn_kernel(*refs, nb, tm, n_sub, has_state):
    h_ref, g_ref, wup_ref, wc_ref, wdn_ref = refs[:5]
    buf_ref = refs[5] if has_state else None
    o_ref, st_ref, carry_scr = refs[5 + int(has_state):]
    i = pl.program_id(1)
    ts = tm // n_sub

    @pl.when(i == 0)
    def _init():
        carry_scr[...] = jnp.zeros_like(carry_scr)
        if has_state:
            for b in range(nb):
                u_m2, u_m1 = buf_ref[b, 0:1, :], buf_ref[b, 1:2, :]
                carry_scr[b, 0:1, :] = u_m1 * wc_ref[0:1, :]
                carry_scr[b, 1:2, :] = u_m1 * wc_ref[1:2, :] + u_m2 * wc_ref[0:1, :]
                carry_scr[b, 2:4, :] = buf_ref[b]

    def delayed(x, carry_row, cols):
        rolled = pltpu.roll(x, 1, axis=0)
        first_row = lax.broadcasted_iota(jnp.int32, (SUBLANES, x.shape[1]), 0) == 0
        pieces = []
        for b in range(nb):
            r0 = b * ts
            pieces.append(jnp.where(first_row, carry_scr[b, carry_row:carry_row + 1, cols],
                                    rolled[r0:r0 + SUBLANES]))
            pieces.append(rolled[r0 + SUBLANES:r0 + ts])
        return jnp.concatenate(pieces, axis=0)

    def conv_act(xn, lo, hi):
        conv = []
        for base in (0, D_FF):
            cols = slice(base + lo, base + hi)
            u = jnp.dot(xn, wup_ref[:, cols], preferred_element_type=F32)
            first = u * wc_ref[0:1, cols]
            mid = u * wc_ref[1:2, cols] + delayed(first, 0, cols)
            y = delayed(mid, 1, cols) + u * wc_ref[2:3, cols]
            for b in range(nb):
                last = (b + 1) * ts - 1
                carry_scr[b, 0:1, cols] = first[last:last + 1]
                carry_scr[b, 1:2, cols] = mid[last:last + 1]
                carry_scr[b, 2:4, cols] = u[last - 1:last + 1]
            conv.append(y)
        return (_silu(conv[0]) * conv[1]).astype(BF16)

    def down(acc, act, lo, hi):
        return acc + jnp.dot(act, wdn_ref[lo:hi, :], preferred_element_type=F32)

    n_ch = len(FFN_COL_CHUNKS)
    for sub in range(n_sub):
        t0 = sub * ts
        hs = _block_rows(h_ref, nb, t0, ts)
        xn = _rms(hs, g_ref[...]).astype(BF16)
        acc = hs
        act = conv_act(xn, *FFN_COL_CHUNKS[0])
        for k in range(n_ch):
            nxt = conv_act(xn, *FFN_COL_CHUNKS[k + 1]) if k + 1 < n_ch else None
            acc = down(acc, act, *FFN_COL_CHUNKS[k])
            act = nxt
        for b in range(nb):
            o_ref[b, t0:t0 + ts, :] = acc[b * ts:(b + 1) * ts]

    @pl.when(i == pl.num_programs(1) - 1)
    def _fin():
        for b in range(nb):
            st_ref[b] = carry_scr[b, 2:4, :]


def _ffn(h, l, p, state):
    B, T, _ = h.shape
    nb, tm, n_sub = _sub_tiles(B, T)
    assert (tm // n_sub) % (2 * SUBLANES) == 0
    tok = _tok_spec(nb, tm, D_MODEL)
    in_specs = [tok, _lspec(l, (1, D_MODEL)), _lspec(l, (D_MODEL, 2 * D_FF), single_buffer=True),
                _lspec(l, (FFN_CONV, 2 * D_FF)), _lspec(l, (D_FF, D_MODEL), single_buffer=True)]
    args = [h, p['norm_ffn_g'], p['w_up'], p['w_ffn_conv'], p['w_down']]
    if state is not None:
        in_specs.append(pl.BlockSpec((None, nb, FFN_CONV - 1, 2 * D_FF), lambda b, i: (l, b, 0, 0)))
        args.append(state)
    return pl.pallas_call(
        functools.partial(_ffn_kernel, nb=nb, tm=tm, n_sub=n_sub, has_state=state is not None),
        grid=(B // nb, T // tm),
        in_specs=in_specs,
        out_specs=[tok, pl.BlockSpec((nb, FFN_CONV - 1, 2 * D_FF), lambda b, i: (b, 0, 0))],
        out_shape=[jax.ShapeDtypeStruct(h.shape, F32),
                   jax.ShapeDtypeStruct((B, FFN_CONV - 1, 2 * D_FF), F32)],
        scratch_shapes=[pltpu.VMEM((nb, SUBLANES, 2 * D_FF), F32)],
        compiler_params=_params(("arbitrary", "arbitrary")),
        name="ffn",
    )(*args)


def _layer(x, l, p, mk, mv, lm, gdn_states, ffn_state, swa_cache, keep):
    B, T, _ = x.shape
    qkv, z, swa, sc, ab = _in_proj(x, l, p)
    oa, oc, s_new, sc_new = _gdn(qkv, ab, z, sc, l, p, gdn_states)
    if swa_cache is None:
        ob, kn = _swa_prompt(swa, l, p)
    else:
        ob, kn = _swa_sample(swa, swa_cache[0], swa_cache[1], l, p)
    h = _post(x, oa, ob, oc, mk, mv, lm, l, p)
    h, f_new = _ffn(h, l, p, ffn_state)
    gb_new = qkv[:, :, T - (GDN_CONV - 1):, :].transpose(0, 2, 1, 3).reshape(
        B, GDN_CONV - 1, GDN_CONV_CH)
    sk = kn[:, T - keep:].reshape(B, keep, SWA_KV_HEADS, SWA_HD)
    sv = swa[:, T - keep:, SWA_Q + SWA_KV:].reshape(B, keep, SWA_KV_HEADS, SWA_HD)
    return h, s_new, gb_new, sc_new, f_new, sk, sv


def kernel(x_prompt, x_sample, mem_prompt, state_gdn, state_gdn_conv, cache_swa_k, cache_swa_v, state_sc_conv, cache_mem_k, cache_mem_v, state_ffn_conv, norm_mix_g, w_in, w_gdn_conv, gdn_a_log, gdn_dt_bias, gdn_norm_g, swa_q_norm_g, swa_k_norm_g, swa_sinks, w_sc_conv, w_o, norm_mem_g, mem_in_norm_g, w_mq, w_mk, w_mv, mem_q_norm_g, mem_k_norm_g, w_mo, norm_ffn_g, w_up, w_ffn_conv, w_down):
    depth = w_in.shape[0]
    Bp, Tp, _ = x_prompt.shape
    Bs, Ts, _ = x_sample.shape
    keep = min(WINDOW, Tp)
    past = cache_swa_k.shape[2]

    qz = GDN_CONV_CH + GDN_V
    w_in_b = w_in.astype(BF16)
    w_in_r = jnp.concatenate(
        [w_in_b[:, :, :qz], w_in_b[:, :, qz + 2 * GDN_HEADS:], w_in_b[:, :, qz:qz + 2 * GDN_HEADS],
         jnp.zeros((depth, D_MODEL, AB_W - 2 * GDN_HEADS), BF16)], axis=-1)
    hp = jnp.zeros((depth, SUBLANES, LANES), F32)
    hp = hp.at[:, 0, :GDN_HEADS].set(gdn_a_log).at[:, 1, :GDN_HEADS].set(gdn_dt_bias)
    sinks = jnp.zeros((depth, 1, LANES), F32).at[:, 0, :SWA_HEADS].set(swa_sinks)
    row = lambda a: a[:, None, :]
    p = {
        'norm_mix_g': row(norm_mix_g), 'w_in': w_in_r, 'w_gdn_conv': w_gdn_conv, 'hp': hp,
        'gdn_norm_g': row(gdn_norm_g), 'swa_q_norm_g': row(swa_q_norm_g),
        'swa_k_norm_g': row(swa_k_norm_g), 'swa_sinks': sinks,
        'swa_q_norm_g_tiled': row(jnp.tile(swa_q_norm_g, (1, SWA_HEADS))),
        'swa_k_norm_g_tiled': row(jnp.tile(swa_k_norm_g, (1, SWA_KV_HEADS))), 'w_sc_conv': w_sc_conv,
        'w_o': w_o.astype(BF16), 'norm_mem_g': row(norm_mem_g), 'mem_in_norm_g': row(mem_in_norm_g),
        'w_mq': w_mq.astype(BF16), 'w_mk': w_mk.astype(BF16), 'w_mv': w_mv.astype(BF16),
        'mem_q_norm_g': row(mem_q_norm_g), 'mem_k_norm_g': row(mem_k_norm_g),
        'w_mo': w_mo.astype(BF16), 'norm_ffn_g': row(norm_ffn_g), 'w_up': w_up.astype(BF16),
        'w_ffn_conv': w_ffn_conv, 'w_down': w_down.astype(BF16),
    }
    cache_k = cache_swa_k.reshape(depth, Bs, past, SWA_KV)
    cache_v = cache_swa_v.reshape(depth, Bs, past, SWA_KV)
    mem_k_s = cache_mem_k.reshape(depth, Bs, MEM_TOKENS, D_MODEL)
    mem_v_s = cache_mem_v.reshape(depth, Bs, MEM_TOKENS, D_MODEL)

    hp_, hs_ = x_prompt, x_sample
    outs_p = {k: [] for k in ('S', 'gb', 'sk', 'sv', 'sb', 'mk', 'mv', 'fb')}
    outs_s = {k: [] for k in ('S', 'gb', 'sk', 'sv', 'sb', 'fb')}
    for l in range(depth):
        mk, mv = _mem_kv(mem_prompt, l, p)
        hp_, S, gb, sb, fb, sk, sv = _layer(hp_, l, p, mk[None], mv[None], 0, None, None, None, keep)
        outs_p['S'].append(S); outs_p['gb'].append(gb); outs_p['sb'].append(sb); outs_p['fb'].append(fb)
        outs_p['sk'].append(sk); outs_p['sv'].append(sv)
        outs_p['mk'].append(mk.reshape(Bp, MEM_TOKENS, MEM_HEADS, MEM_HD))
        outs_p['mv'].append(mv.reshape(Bp, MEM_TOKENS, MEM_HEADS, MEM_HD))
        hs_, S, gb, sb, fb, sk, sv = _layer(
            hs_, l, p, mem_k_s, mem_v_s, l, (state_gdn, state_gdn_conv, state_sc_conv),
            state_ffn_conv, (cache_k, cache_v), Ts)
        outs_s['S'].append(S); outs_s['gb'].append(gb); outs_s['sb'].append(sb); outs_s['fb'].append(fb)
        outs_s['sk'].append(sk); outs_s['sv'].append(sv)
    st = jnp.stack
    return (hp_, hs_,
            st(outs_p['S']), st(outs_p['gb']), st(outs_p['sk']), st(outs_p['sv']), st(outs_p['sb']),
            st(outs_p['mk']), st(outs_p['mv']), st(outs_p['fb']),
            st(outs_s['S']), st(outs_s['gb']), st(outs_s['sk']), st(outs_s['sv']), st(outs_s['sb']),
            st(outs_s['fb']))
```

```python
import functools

import jax
import jax.numpy as jnp
from jax import lax
from jax.experimental import pallas as pl
from jax.experimental.pallas import tpu as pltpu

F32 = jnp.float32
BF16 = jnp.bfloat16

D_MODEL = 1024
CHUNK = 64
EPS = 1e-6
NEG_INF = -1e30
GDN_HEADS = 4
GDN_DK = 128
GDN_DV = 128
GDN_CONV = 4
GDN_QK = GDN_HEADS * GDN_DK
GDN_V = GDN_HEADS * GDN_DV
GDN_CONV_CH = 2 * GDN_QK + GDN_V
SWA_HEADS = 4
SWA_KV_HEADS = 2
SWA_GROUP = SWA_HEADS // SWA_KV_HEADS
SWA_HD = 64
WINDOW = 128
WIN_CHUNKS = WINDOW // CHUNK
SWA_Q = SWA_HEADS * SWA_HD
SWA_KV = SWA_KV_HEADS * SWA_HD
SWA_W = SWA_Q + 2 * SWA_KV
assert SWA_GROUP == SWA_KV_HEADS == 2 and SWA_KV == 128
SC_WIDTH = 256
SC_CONV = 3
MEM_TOKENS = 256
MEM_HEADS = 4
MEM_HD = D_MODEL // MEM_HEADS
D_FF = 2816
FFN_CONV = 3
LANES = 128
SUBLANES = 8
AB_W = LANES
SC_W = 3 * SC_WIDTH
IN_COLS = (
    (0, GDN_CONV_CH),
    (GDN_CONV_CH, GDN_CONV_CH + GDN_V),
    (GDN_CONV_CH + GDN_V, GDN_CONV_CH + GDN_V + SWA_W),
    (GDN_CONV_CH + GDN_V + SWA_W, GDN_CONV_CH + GDN_V + SWA_W + SC_W),
    (GDN_CONV_CH + GDN_V + SWA_W + SC_W, GDN_CONV_CH + GDN_V + SWA_W + SC_W + AB_W),
)
D_IN_PAD = IN_COLS[-1][1]
D_IN = D_IN_PAD - AB_W + 2 * GDN_HEADS
FFN_COL_CHUNKS = ((0, 1024), (1024, 2048), (2048, D_FF))
GDN_CHUNKS_PER_STEP = 4
SUB_TILES = 2
VMEM_LIMIT = 56 * 1024 * 1024
TOKEN_TILE = 512
SWA_QUERY_BLOCK = 1024
MAX_ROWS_SMALL_SEQ = 128


def _seq_tiles(B, T):
    if T >= TOKEN_TILE:
        return 1, TOKEN_TILE
    nb = max(1, min(B, MAX_ROWS_SMALL_SEQ // T))
    while B % nb:
        nb -= 1
    return nb, T


def _sub_tiles(B, T):
    nb, tm = _seq_tiles(B, T)
    if nb == 1 and T % (SUB_TILES * tm) == 0:
        return nb, SUB_TILES * tm, SUB_TILES
    return nb, tm, 1


def _dot(a, b):
    return jnp.dot(a.astype(BF16), b.astype(BF16), preferred_element_type=F32)


def _dot_nt(a, b):
    return lax.dot_general(a.astype(BF16), b.astype(BF16), (((1,), (1,)), ((), ())),
                           preferred_element_type=F32)


def _dot_tn(a, b):
    return lax.dot_general(a.astype(BF16), b.astype(BF16), (((0,), (0,)), ((), ())),
                           preferred_element_type=F32)


def _rms(x, g):
    return x * lax.rsqrt(jnp.mean(x * x, axis=-1, keepdims=True) + EPS) * g


def _silu(x):
    return x * jax.nn.sigmoid(x)


def _lspec(l, shape, single_buffer=False):
    nd = len(shape)
    mode = dict(pipeline_mode=pl.Buffered(1)) if single_buffer else {}
    return pl.BlockSpec((None,) + tuple(shape), lambda *_: (l,) + (0,) * nd, **mode)


def _params(sem):
    return pltpu.CompilerParams(dimension_semantics=sem, vmem_limit_bytes=VMEM_LIMIT)


def _tok_spec(nb, tm, w):
    return pl.BlockSpec((nb, tm, w), lambda b, i: (b, i, 0))


def _block_rows(ref, nb, t0=0, ts=None):
    ts = ref.shape[-2] - t0 if ts is None else ts
    if len(ref.shape) == 3:
        seqs = [ref[b, t0:t0 + ts, :] for b in range(nb)]
    else:
        seqs = [jnp.concatenate([ref[b, c, t0:t0 + ts, :] for c in range(ref.shape[1])], axis=-1)
                for b in range(nb)]
    return seqs[0] if nb == 1 else jnp.concatenate(seqs, axis=0)


def _in_proj_kernel(x_ref, g_ref, w_ref, qkv_ref, z_ref, swa_ref, sc_ref, ab_ref, w_scr, *, nb, tm):
    @pl.when((pl.program_id(0) == 0) & (pl.program_id(1) == 0))
    def _regroup():
        qz = GDN_CONV_CH + GDN_V
        n_gate = 2 * GDN_HEADS
        w_scr[:, 0:qz] = w_ref[:, 0:qz]
        w_scr[:, qz:IN_COLS[-1][0]] = w_ref[:, qz + n_gate:D_IN]
        w_scr[:, IN_COLS[-1][0]:D_IN_PAD] = jnp.concatenate(
            [w_ref[:, qz:qz + n_gate], jnp.zeros((D_MODEL, AB_W - n_gate), BF16)], axis=1)

    xn = _rms(_block_rows(x_ref, nb), g_ref[...]).astype(BF16)
    for o_ref, (lo, hi) in zip((qkv_ref, z_ref, swa_ref, sc_ref, ab_ref), IN_COLS):
        res = jnp.dot(xn, w_scr[:, lo:hi], preferred_element_type=F32)
        for b in range(nb):
            rows = res[b * tm:(b + 1) * tm]
            if len(o_ref.shape) == 4:
                for c in range(o_ref.shape[1]):
                    o_ref[b, c] = rows[:, c * LANES:(c + 1) * LANES]
            else:
                o_ref[b] = rows


def _in_proj(xs, l, p):
    B, T, _ = xs.shape
    nb, tm = _seq_tiles(B, T)
    widths = [hi - lo for lo, hi in IN_COLS]
    slabbed = (True, True, False, False, False)
    out_specs, out_shape = [], []
    for wd, sl in zip(widths, slabbed):
        if sl:
            out_specs.append(pl.BlockSpec((nb, wd // LANES, tm, LANES), lambda b, i: (b, 0, i, 0)))
            out_shape.append(jax.ShapeDtypeStruct((B, wd // LANES, T, LANES), F32))
        else:
            out_specs.append(pl.BlockSpec((nb, tm, wd), lambda b, i: (b, i, 0)))
            out_shape.append(jax.ShapeDtypeStruct((B, T, wd), F32))
    return pl.pallas_call(
        functools.partial(_in_proj_kernel, nb=nb, tm=tm),
        grid=(B // nb, T // tm),
        in_specs=[_tok_spec(nb, tm, D_MODEL), _lspec(l, (1, D_MODEL)),
                  _lspec(l, (D_MODEL, D_IN), single_buffer=True)],
        out_specs=out_specs,
        out_shape=out_shape,
        scratch_shapes=[pltpu.VMEM((D_MODEL, D_IN_PAD), BF16)],
        compiler_params=_params(("arbitrary", "arbitrary")),
        name="in_proj",
    )(xs, p['norm_mix_g'], p['w_in'])


def _cumsum_time(groups):
    prefix = [groups[0]]
    for g in groups[1:]:
        prefix.append(prefix[-1] + g)
    sub = lax.broadcasted_iota(jnp.int32, groups[0].shape, 0)
    run = prefix[-1]
    s = 1
    while s < SUBLANES:
        run = run + jnp.where(sub >= s, pltpu.roll(run, s, axis=0), 0.0)
        s *= 2
    before = jnp.where(sub >= 1, pltpu.roll(run, 1, axis=0), 0.0)
    return [p + before for p in prefix]


def _gdn_kernel(*refs, bb, C, nc, has_state):
    qkv_ref, ab_ref, z_ref, sc_ref, wconv_ref, wsc_ref, hp_ref, gn_ref = refs[:8]
    state_refs = refs[8:11] if has_state else ()
    oa_ref, oc_ref, sout_ref, scout_ref, s_scr, xp_scr, scp_scr = refs[8 + len(state_refs):]
    n = pl.program_id(1)
    pad = SUBLANES
    L = C // SUBLANES
    n_slabs = GDN_CONV_CH // LANES

    def time_of(r):
        return (r % SUBLANES) * L + r // SUBLANES

    def token_row(t):
        return (t % L) * SUBLANES + t // L

    @pl.when(n == 0)
    def _init():
        xp_scr[...] = jnp.zeros_like(xp_scr)
        scp_scr[...] = jnp.zeros_like(scp_scr)
        if has_state:
            s0_ref, gbuf_ref, scbuf_ref = state_refs
            s_scr[...] = s0_ref[...]
            for r in range(GDN_CONV - 1):
                row = token_row(C - (GDN_CONV - 1) + r)
                for b in range(bb):
                    for m in range(n_slabs):
                        xp_scr[b, m, row:row + 1, :] = gbuf_ref[b, r:r + 1, m * LANES:(m + 1) * LANES]
            scp_scr[:, pad - (SC_CONV - 1):pad, :] = scbuf_ref[...]
        else:
            s_scr[...] = jnp.zeros_like(s_scr)

    t_row = time_of(lax.broadcasted_iota(jnp.int32, (C, C), 0))
    t_col = time_of(lax.broadcasted_iota(jnp.int32, (C, C), 1))
    incl = t_row >= t_col
    strict = t_row > t_col
    sub = lax.broadcasted_iota(jnp.int32, (SUBLANES, LANES), 0)
    neg_a = -jnp.exp(hp_ref[0:1, :])
    dt_b = hp_ref[1:2, :]
    n_sq = C.bit_length() - 2

    R = nc * C

    def chunk_groups(window):
        return [window(j) for j in range(L)]

    def shifted(cur, prev, k, cache):
        out = []
        for j in range(L):
            d = j - k
            if d >= 0:
                out.append(cur[d])
                continue
            m = (-d + L - 1) // L
            src = d + m * L
            if (src, m) not in cache:
                cache[src, m] = pltpu.roll(
                    jnp.where(sub >= SUBLANES - m, prev[src], cur[src]), m, axis=0)
            out.append(cache[src, m])
        return jnp.concatenate(out, axis=0)

    ys, gcs, gcts, betas = {}, {}, {}, {}
    for b in range(bb):
        for m in range(n_slabs):
            lanes = slice(m * LANES, (m + 1) * LANES)
            prev = [xp_scr[b, m, j * SUBLANES:(j + 1) * SUBLANES, :] for j in range(L)]
            for c in range(nc):
                cur = chunk_groups(
                    lambda j: qkv_ref[b, m, pl.ds(c * C + j, SUBLANES, stride=L), :])
                cache = {}
                y = shifted(cur, prev, 3, cache) * wconv_ref[0:1, lanes]
                y = y + shifted(cur, prev, 2, cache) * wconv_ref[1:2, lanes]
                y = y + shifted(cur, prev, 1, cache) * wconv_ref[2:3, lanes]
                y = y + jnp.concatenate(cur, axis=0) * wconv_ref[3:4, lanes]
                ys[b, c, m] = _silu(y)
                prev = cur
            xp_scr[b, m] = jnp.concatenate(prev, axis=0)

        for c in range(nc):
            ab = jnp.concatenate(chunk_groups(
                lambda j: ab_ref[b, pl.ds(c * C + j, SUBLANES, stride=L), :]), axis=0)
            t = ab + dt_b
            g = neg_a * (jnp.maximum(t, 0.0) + jnp.log1p(jnp.exp(-jnp.abs(t))))
            betas[b, c] = jax.nn.sigmoid(ab)
            gc = jnp.concatenate(_cumsum_time(
                [g[j * SUBLANES:(j + 1) * SUBLANES] for j in range(L)]), axis=0)
            gcs[b, c] = gc
            gcts[b, c] = jnp.transpose(jnp.concatenate(
                [gc, jnp.zeros((LANES - C, LANES), F32)], axis=0))

        scv = sc_ref[b]
        prod = scv[:, SC_WIDTH:2 * SC_WIDTH] * scv[:, 2 * SC_WIDTH:3 * SC_WIDTH]
        pext = jnp.concatenate([scp_scr[b], prod], axis=0)
        cu = pltpu.roll(pext, 2, axis=0)[pad:] * wsc_ref[0:1, :]
        cu = cu + pltpu.roll(pext, 1, axis=0)[pad:] * wsc_ref[1:2, :]
        cu = cu + prod * wsc_ref[2:3, :]
        scp_scr[b] = prod[R - pad:R, :]
        oc_ref[b] = scv[:, 0:SC_WIDTH] * cu

    problems = [(c, b, h) for c in range(nc) for b in range(bb) for h in range(GDN_HEADS)]
    every = range(len(problems))
    qs, ks, vs, kbs, bts, egs, e_incls, e_stricts, kds, gls = ([] for _ in range(10))
    for c, b, h in problems:
        gc = gcs[b, c]
        q = ys[b, c, h]
        k = ys[b, c, GDN_HEADS + h]
        q = q * lax.rsqrt(jnp.sum(q * q, axis=-1, keepdims=True) + EPS) * (GDN_DK ** -0.5)
        k = k * lax.rsqrt(jnp.sum(k * k, axis=-1, keepdims=True) + EPS)
        gci = gc[:, h:h + 1]
        gcj = gcts[b, c][h:h + 1, 0:C]
        g_last = gc[C - 1:C, h:h + 1]
        e_incl = jnp.exp(jnp.where(incl, gci - gcj, NEG_INF))
        qs.append(q)
        ks.append(k)
        kbs.append(k.astype(BF16))
        vs.append(ys[b, c, 2 * GDN_HEADS + h])
        bts.append(betas[b, c][:, GDN_HEADS + h:GDN_HEADS + h + 1])
        egs.append(jnp.exp(gci))
        e_incls.append(e_incl)
        e_stricts.append(jnp.where(strict, e_incl, 0.0))
        kds.append(k * jnp.exp(g_last - gci))
        gls.append(jnp.exp(g_last))

    kks = [_dot_nt(kbs[i], kbs[i]) for i in every]
    qks = [_dot_nt(qs[i], kbs[i]) * e_incls[i] for i in every]
    p_pows = [-(bts[i] * kks[i] * e_stricts[i]) for i in every]
    n_accs = list(p_pows)
    for _ in range(n_sq):
        pbs = [p.astype(BF16) for p in p_pows]
        p_pows = [_dot(pb, pb) for pb in pbs]
        n_accs = [n_accs[i] + p_pows[i] + _dot(p_pows[i], n_accs[i]) for i in every]
    rhss = [jnp.concatenate([bts[i] * vs[i], (bts[i] * egs[i]) * ks[i]], axis=-1) for i in every]
    sols = [rhss[i] + _dot(n_accs[i], rhss[i]) for i in every]

    pairs = [(b, h) for b in range(bb) for h in range(GDN_HEADS)]
    here = range(len(pairs))
    states = [s_scr[b, h] for b, h in pairs]
    for c in range(nc):
        base = c * len(pairs)
        sbs = [s.astype(BF16) for s in states]
        o_inter = [_dot(qs[base + j] * egs[base + j], sbs[j]) for j in here]
        us = [sols[base + j][:, :GDN_DV] - _dot(sols[base + j][:, GDN_DV:], sbs[j]) for j in here]
        outs = [o_inter[j] + _dot(qks[base + j], us[j]) for j in here]
        states = [gls[base + j] * states[j] + _dot_tn(kds[base + j], us[j]) for j in here]
        for j, (b, h) in enumerate(pairs):
            zh = jnp.concatenate(chunk_groups(
                lambda i: z_ref[b, h, pl.ds(c * C + i, SUBLANES, stride=L), :]), axis=0)
            res = _rms(outs[j], gn_ref[...]) * _silu(zh)
            for i in range(L):
                oa_ref[b, h, pl.ds(c * C + i, SUBLANES, stride=L), :] = res[
                    i * SUBLANES:(i + 1) * SUBLANES]
    for j, (b, h) in enumerate(pairs):
        s_scr[b, h] = states[j]

    @pl.when(n == pl.num_programs(1) - 1)
    def _fin():
        sout_ref[...] = s_scr[...]
        scout_ref[...] = scp_scr[:, pad - (SC_CONV - 1):pad, :]


def _gdn(qkv, ab, z, sc, l, p, states):
    B, n_slabs, T, _ = qkv.shape
    C = CHUNK if T % CHUNK == 0 else T
    assert C % SUBLANES == 0
    bb = 2 if B % 2 == 0 else 1
    nc = GDN_CHUNKS_PER_STEP if (T // C) % GDN_CHUNKS_PER_STEP == 0 else 1
    tok = lambda w: pl.BlockSpec((bb, nc * C, w), lambda i, n: (i, n, 0))
    slabs = lambda k: pl.BlockSpec((bb, k, nc * C, LANES), lambda i, n: (i, 0, n, 0))
    per_b = lambda *s: pl.BlockSpec((bb,) + s, lambda i, n: (i,) + (0,) * len(s))
    per_lb = lambda *s: pl.BlockSpec((None, bb) + s, lambda i, n: (l, i) + (0,) * len(s))
    in_specs = [slabs(n_slabs), tok(AB_W), slabs(GDN_HEADS), tok(SC_W),
                _lspec(l, (GDN_CONV, GDN_CONV_CH)), _lspec(l, (SC_CONV, SC_WIDTH)),
                _lspec(l, (SUBLANES, LANES)), _lspec(l, (1, GDN_DV))]
    args = [qkv, ab, z, sc, p['w_gdn_conv'], p['w_sc_conv'], p['hp'], p['gdn_norm_g']]
    if states is not None:
        in_specs += [per_lb(GDN_HEADS, GDN_DK, GDN_DV), per_lb(GDN_CONV - 1, GDN_CONV_CH),
                     per_lb(SC_CONV - 1, SC_WIDTH)]
        args += list(states)
    return pl.pallas_call(
        functools.partial(_gdn_kernel, bb=bb, C=C, nc=nc, has_state=states is not None),
        grid=(B // bb, T // (nc * C)),
        in_specs=in_specs,
        out_specs=[slabs(GDN_HEADS), tok(SC_WIDTH),
                   per_b(GDN_HEADS, GDN_DK, GDN_DV), per_b(SC_CONV - 1, SC_WIDTH)],
        out_shape=[jax.ShapeDtypeStruct((B, GDN_HEADS, T, GDN_DV), F32),
                   jax.ShapeDtypeStruct((B, T, SC_WIDTH), F32),
                   jax.ShapeDtypeStruct((B, GDN_HEADS, GDN_DK, GDN_DV), F32),
                   jax.ShapeDtypeStruct((B, SC_CONV - 1, SC_WIDTH), F32)],
        scratch_shapes=[pltpu.VMEM((bb, GDN_HEADS, GDN_DK, GDN_DV), F32),
                        pltpu.VMEM((bb, n_slabs, C, LANES), F32),
                        pltpu.VMEM((bb, SUBLANES, SC_WIDTH), F32)],
        compiler_params=_params(("arbitrary", "arbitrary")),
        name="gdn",
    )(*args)


def _swa_norm_k(x, gk):
    return jnp.concatenate(
        [_rms(x[:, SWA_Q + i * SWA_HD:SWA_Q + (i + 1) * SWA_HD], gk) for i in range(SWA_KV_HEADS)],
        axis=-1)


def _swa_norm_q(x, gq):
    return [_rms(x[:, h * SWA_HD:(h + 1) * SWA_HD], gq) for h in range(SWA_HEADS)]


def _group_queries(qn, g, r0, r1, sinks):
    heads = range(g * SWA_GROUP, (g + 1) * SWA_GROUP)
    q = jnp.concatenate([qn[h][r0:r1] for h in heads], axis=0)
    sink = jnp.concatenate(
        [jnp.broadcast_to(sinks[:, h:h + 1], (r1 - r0, 1)) for h in heads], axis=0)
    return q, sink


def _attend_many(problems):
    scores = [_dot_nt(q, k) * (SWA_HD ** -0.5) for q, k, _, _, _ in problems]
    probs = []
    for s, (_, _, _, valid, sink) in zip(scores, problems):
        if valid is not None:
            s = jnp.where(valid, s, NEG_INF)
        m = jnp.maximum(jnp.max(s, axis=-1, keepdims=True), sink)
        p = jnp.exp(s - m)
        probs.append(p / (jnp.sum(p, axis=-1, keepdims=True) + jnp.exp(sink - m)))
    return [_dot(p, pr[2]) for p, pr in zip(probs, problems)]


def _ungroup(outs, rows):
    return jnp.concatenate(
        [o[j * rows:(j + 1) * rows] for o in outs for j in range(SWA_GROUP)], axis=-1)


def _split_bf16(x):
    hi = x.astype(BF16)
    return hi, (x - hi.astype(F32)).astype(BF16)


def _sum_dot(x, ones):
    hi, lo = _split_bf16(x)
    return (jnp.dot(hi, ones, preferred_element_type=F32)
            + jnp.dot(lo, ones, preferred_element_type=F32))


def _head_rms(x, g_tiled):
    w = x.shape[1]
    same_head = (lax.broadcasted_iota(jnp.int32, (w, w), 0) // SWA_HD
                 == lax.broadcasted_iota(jnp.int32, (w, w), 1) // SWA_HD)
    ss = _sum_dot(x * x, jnp.where(same_head, 1.0, 0.0).astype(BF16))
    return x * lax.rsqrt(ss * (1.0 / SWA_HD) + EPS) * g_tiled


def _swa_prompt_kernel(x_ref, halo_ref, gq_ref, gk_ref, sink_ref, o_ref, kn_ref, *, qb):
    i = pl.program_id(1)
    band = (WIN_CHUNKS + 1) * CHUNK
    n_chunks = qb // CHUNK
    x = x_ref[...]
    hx = halo_ref[...]
    kcols = slice(SWA_Q, SWA_Q + SWA_KV)
    vcols = slice(SWA_Q + SWA_KV, SWA_W)
    qn = _head_rms(x[:, 0:SWA_Q], gq_ref[...])
    k_all = _head_rms(jnp.concatenate([hx[:, kcols], x[:, kcols]], axis=0), gk_ref[...])
    v_all = jnp.concatenate([hx[:, vcols], x[:, vcols]], axis=0)
    kn_ref[...] = k_all[WINDOW:]

    lane_half = lax.broadcasted_iota(jnp.int32, (1, SWA_KV), 1) // SWA_HD

    def variants(a):
        rot = pltpu.roll(a, SWA_HD, axis=1)
        return [[jnp.where(lane_half == j, a if j == g else rot, 0.0).astype(BF16)
                 for j in range(SWA_GROUP)] for g in range(SWA_KV_HEADS)]

    k_var, v_var = variants(k_all), variants(v_all)
    ones = jnp.ones((band, SWA_KV), BF16)
    local_chunk = lax.broadcasted_iota(jnp.int32, (1, band), 1) // CHUNK
    problems = [(c, g, j) for c in range(n_chunks) for g in range(SWA_KV_HEADS)
                for j in range(SWA_GROUP)]
    scores = []
    for c, g, j in problems:
        q = qn[c * CHUNK:(c + 1) * CHUNK, g * SWA_KV:(g + 1) * SWA_KV]
        scores.append(_dot_nt(q, k_var[g][j][c * CHUNK:c * CHUNK + band]) * (SWA_HD ** -0.5))
    ps, tails = [], []
    for s, (c, g, j) in zip(scores, problems):
        valid = local_chunk + (i * n_chunks + c - WIN_CHUNKS) >= 0
        s = jnp.where(valid, s, NEG_INF)
        sink = sink_ref[:, g * SWA_GROUP + j:g * SWA_GROUP + j + 1]
        m = jnp.maximum(jnp.max(s, axis=-1, keepdims=True), sink)
        ps.append(jnp.exp(s - m))
        tails.append(jnp.exp(sink - m))
    outs = []
    for p, tail, (c, g, j) in zip(ps, tails, problems):
        denom = _sum_dot(p, ones) + tail
        outs.append(_dot(p, v_var[g][j][c * CHUNK:c * CHUNK + band]) / denom)
    for c in range(n_chunks):
        for g in range(SWA_KV_HEADS):
            k0 = (c * SWA_KV_HEADS + g) * SWA_GROUP
            o_ref[c * CHUNK:(c + 1) * CHUNK, g * SWA_KV:(g + 1) * SWA_KV] = sum(
                outs[k0 + 1:k0 + SWA_GROUP], outs[k0])


def _swa_prompt(swa, l, p):
    B, T, _ = swa.shape
    qb = min(SWA_QUERY_BLOCK, T)
    assert T % qb == 0 and qb % WINDOW == 0
    halo_per_block = qb // WINDOW
    return pl.pallas_call(
        functools.partial(_swa_prompt_kernel, qb=qb),
        grid=(B, T // qb),
        in_specs=[pl.BlockSpec((None, qb, SWA_W), lambda b, i: (b, i, 0)),
                  pl.BlockSpec((None, WINDOW, SWA_W),
                               lambda b, i: (b, jnp.maximum(i * halo_per_block - 1, 0), 0)),
                  _lspec(l, (1, SWA_Q)), _lspec(l, (1, SWA_KV)), _lspec(l, (1, LANES))],
        out_specs=[pl.BlockSpec((None, qb, SWA_Q), lambda b, i: (b, i, 0)),
                   pl.BlockSpec((None, qb, SWA_KV), lambda b, i: (b, i, 0))],
        out_shape=[jax.ShapeDtypeStruct((B, T, SWA_Q), F32),
                   jax.ShapeDtypeStruct((B, T, SWA_KV), F32)],
        compiler_params=_params(("arbitrary", "arbitrary")),
        name="swa_prompt",
    )(swa, swa, p['swa_q_norm_g_tiled'], p['swa_k_norm_g_tiled'], p['swa_sinks'])


def _swa_sample_kernel(x_ref, kp_ref, vp_ref, gq_ref, gk_ref, sink_ref, o_ref, kn_ref, *, bb, T):
    problems = []
    for b in range(bb):
        x = x_ref[b]
        kn = _swa_norm_k(x, gk_ref[...])
        kn_ref[b] = kn
        k_all = jnp.concatenate([kp_ref[b], kn], axis=0)
        v_all = jnp.concatenate([vp_ref[b], x[:, SWA_Q + SWA_KV:SWA_W]], axis=0)
        qn = _swa_norm_q(x, gq_ref[...])
        for g in range(SWA_KV_HEADS):
            q, sink = _group_queries(qn, g, 0, T, sink_ref[...])
            cols = slice(g * SWA_HD, (g + 1) * SWA_HD)
            problems.append((q, k_all[:, cols], v_all[:, cols], None, sink))
    outs = _attend_many(problems)
    for b in range(bb):
        o_ref[b] = _ungroup(outs[b * SWA_KV_HEADS:(b + 1) * SWA_KV_HEADS], T)


def _swa_sample(swa, k_past, v_past, l, p):
    B, T, _ = swa.shape
    past = k_past.shape[2]
    bb = 4 if B % 4 == 0 else 1
    blk = lambda r, w: pl.BlockSpec((bb, r, w), lambda b: (b, 0, 0))
    cache = pl.BlockSpec((None, bb, past, SWA_KV), lambda b: (l, b, 0, 0))
    return pl.pallas_call(
        functools.partial(_swa_sample_kernel, bb=bb, T=T),
        grid=(B // bb,),
        in_specs=[blk(T, SWA_W), cache, cache,
                  _lspec(l, (1, SWA_HD)), _lspec(l, (1, SWA_HD)), _lspec(l, (1, LANES))],
        out_specs=[blk(T, SWA_Q), blk(T, SWA_KV)],
        out_shape=[jax.ShapeDtypeStruct((B, T, SWA_Q), F32),
                   jax.ShapeDtypeStruct((B, T, SWA_KV), F32)],
        compiler_params=_params(("arbitrary",)),
        name="swa_sample",
    )(swa, k_past, v_past, p['swa_q_norm_g'], p['swa_k_norm_g'], p['swa_sinks'])


def _mem_kv_kernel(m_ref, g_ref, wk_ref, wv_ref, gk_ref, k_ref, v_ref):
    m = _rms(m_ref[...], g_ref[...]).astype(BF16)
    k = jnp.dot(m, wk_ref[...], preferred_element_type=F32)
    k_ref[...] = jnp.concatenate(
        [_rms(k[:, h * MEM_HD:(h + 1) * MEM_HD], gk_ref[...]) for h in range(MEM_HEADS)], axis=-1)
    v_ref[...] = jnp.dot(m, wv_ref[...], preferred_element_type=F32)


def _mem_kv(mem, l, p):
    B, M, _ = mem.shape
    blk = pl.BlockSpec((None, M, D_MODEL), lambda b: (b, 0, 0))
    return pl.pallas_call(
        _mem_kv_kernel,
        grid=(B,),
        in_specs=[blk, _lspec(l, (1, D_MODEL)), _lspec(l, (D_MODEL, D_MODEL)),
                  _lspec(l, (D_MODEL, D_MODEL)), _lspec(l, (1, MEM_HD))],
        out_specs=[blk, blk],
        out_shape=[jax.ShapeDtypeStruct((B, M, D_MODEL), F32)] * 2,
        compiler_params=_params(("arbitrary",)),
        name="mem_kv",
    )(mem, p['mem_in_norm_g'], p['w_mk'], p['w_mv'], p['mem_k_norm_g'])


def _post_kernel(x_ref, a_ref, b_ref, c_ref, wo_ref, gm_ref, wq_ref, gq_ref, mk_ref, mv_ref,
                 wmo_ref, o_ref, *, nb, tm, n_sub):
    ts = tm // n_sub
    probs = [(b, slice(hd * MEM_HD, (hd + 1) * MEM_HD)) for b in range(nb) for hd in range(MEM_HEADS)]
    hs, scores = [], []
    for sub in range(n_sub):
        rows = lambda ref: _block_rows(ref, nb, sub * ts, ts)
        h = rows(x_ref)
        h = h + (jnp.dot(rows(a_ref).astype(BF16), wo_ref[0:GDN_V, :], preferred_element_type=F32)
                 + jnp.dot(rows(b_ref).astype(BF16), wo_ref[GDN_V:GDN_V + SWA_Q, :],
                           preferred_element_type=F32)
                 + jnp.dot(rows(c_ref).astype(BF16), wo_ref[GDN_V + SWA_Q:, :],
                           preferred_element_type=F32))
        hs.append(h)
        hn = _rms(h, gm_ref[...]).astype(BF16)
        q = jnp.dot(hn, wq_ref[...], preferred_element_type=F32)
        qhs = [_rms(q[b * ts:(b + 1) * ts, sl], gq_ref[...]) for b, sl in probs]
        scores.append([_dot_nt(qh, mk_ref[b, :, sl]) * (MEM_HD ** -0.5)
                       for qh, (b, sl) in zip(qhs, probs)])
    ps = []
    for sub in range(n_sub):
        ps.append([])
        for s in scores[sub]:
            p = jnp.exp(s - jnp.max(s, axis=-1, keepdims=True))
            ps[sub].append(p / jnp.sum(p, axis=-1, keepdims=True))
    for sub in range(n_sub):
        outs = [_dot(p, mv_ref[b, :, sl]) for p, (b, sl) in zip(ps[sub], probs)]
        o = jnp.concatenate(
            [jnp.concatenate(outs[b * MEM_HEADS:(b + 1) * MEM_HEADS], axis=-1) for b in range(nb)],
            axis=0).astype(BF16)
        res = hs[sub] + jnp.dot(o, wmo_ref[...], preferred_element_type=F32)
        for b in range(nb):
            o_ref[b, sub * ts:(sub + 1) * ts, :] = res[b * ts:(b + 1) * ts]


def _post(xs, oa, ob, oc, mk, mv, lm, l, p):
    B, T, _ = xs.shape
    nb, tm, n_sub = _sub_tiles(B, T)
    tok = lambda w: _tok_spec(nb, tm, w)
    mem = pl.BlockSpec((None, nb, MEM_TOKENS, D_MODEL), lambda b, i: (lm, b, 0, 0))
    sq = _lspec(l, (D_MODEL, D_MODEL), single_buffer=True)
    return pl.pallas_call(
        functools.partial(_post_kernel, nb=nb, tm=tm, n_sub=n_sub),
        grid=(B // nb, T // tm),
        in_specs=[tok(D_MODEL),
                  pl.BlockSpec((nb, GDN_HEADS, tm, GDN_DV), lambda b, i: (b, 0, i, 0)),
                  tok(SWA_Q), tok(SC_WIDTH), sq,
                  _lspec(l, (1, D_MODEL)), sq, _lspec(l, (1, MEM_HD)), mem, mem, sq],
        out_specs=tok(D_MODEL),
        out_shape=jax.ShapeDtypeStruct((B, T, D_MODEL), F32),
        compiler_params=_params(("arbitrary", "arbitrary")),
        name="post",
    )(xs, oa, ob, oc, p['w_o'], p['norm_mem_g'], p['w_mq'], p['mem_q_norm_g'], mk, mv, p['w_mo'])


def _ffn_kernel(*refs, nb, tm, n_sub, has_state):
    h_ref, g_ref, wup_ref, wc_ref, wdn_ref = refs[:5]
    buf_ref = refs[5] if has_state else None
    o_ref, st_ref, carry_scr = refs[5 + int(has_state):]
    i = pl.program_id(1)
    ts = tm // n_sub

    @pl.when(i == 0)
    def _init():
        carry_scr[...] = jnp.zeros_like(carry_scr)
        if has_state:
            for b in range(nb):
                u_m2, u_m1 = buf_ref[b, 0:1, :], buf_ref[b, 1:2, :]
                carry_scr[b, 0:1, :] = u_m1 * wc_ref[0:1, :]
                carry_scr[b, 1:2, :] = u_m1 * wc_ref[1:2, :] + u_m2 * wc_ref[0:1, :]
                carry_scr[b, 2:4, :] = buf_ref[b]

    def delayed(x, carry_row, cols):
        rolled = pltpu.roll(x, 1, axis=0)
        first_row = lax.broadcasted_iota(jnp.int32, (SUBLANES, x.shape[1]), 0) == 0
        pieces = []
        for b in range(nb):
            r0 = b * ts
            pieces.append(jnp.where(first_row, carry_scr[b, carry_row:carry_row + 1, cols],
                                    rolled[r0:r0 + SUBLANES]))
            pieces.append(rolled[r0 + SUBLANES:r0 + ts])
        return jnp.concatenate(pieces, axis=0)

    def conv_act(xn, lo, hi):
        conv = []
        for base in (0, D_FF):
            cols = slice(base + lo, base + hi)
            u = jnp.dot(xn, wup_ref[:, cols], preferred_element_type=F32)
            first = u * wc_ref[0:1, cols]
            mid = u * wc_ref[1:2, cols] + delayed(first, 0, cols)
            y = delayed(mid, 1, cols) + u * wc_ref[2:3, cols]
            for b in range(nb):
                last = (b + 1) * ts - 1
                carry_scr[b, 0:1, cols] = first[last:last + 1]
                carry_scr[b, 1:2, cols] = mid[last:last + 1]
                carry_scr[b, 2:4, cols] = u[last - 1:last + 1]
            conv.append(y)
        return (_silu(conv[0]) * conv[1]).astype(BF16)

    def down(acc, act, lo, hi):
        return acc + jnp.dot(act, wdn_ref[lo:hi, :], preferred_element_type=F32)

    n_ch = len(FFN_COL_CHUNKS)
    for sub in range(n_sub):
        t0 = sub * ts
        hs = _block_rows(h_ref, nb, t0, ts)
        xn = _rms(hs, g_ref[...]).astype(BF16)
        acc = hs
        act = conv_act(xn, *FFN_COL_CHUNKS[0])
        for k in range(n_ch):
            nxt = conv_act(xn, *FFN_COL_CHUNKS[k + 1]) if k + 1 < n_ch else None
            acc = down(acc, act, *FFN_COL_CHUNKS[k])
            act = nxt
        for b in range(nb):
            o_ref[b, t0:t0 + ts, :] = acc[b * ts:(b + 1) * ts]

    @pl.when(i == pl.num_programs(1) - 1)
    def _fin():
        for b in range(nb):
            st_ref[b] = carry_scr[b, 2:4, :]


def _ffn(h, l, p, state):
    B, T, _ = h.shape
    nb, tm, n_sub = _sub_tiles(B, T)
    assert (tm // n_sub) % (2 * SUBLANES) == 0
    tok = _tok_spec(nb, tm, D_MODEL)
    in_specs = [tok, _lspec(l, (1, D_MODEL)), _lspec(l, (D_MODEL, 2 * D_FF), single_buffer=True),
                _lspec(l, (FFN_CONV, 2 * D_FF)), _lspec(l, (D_FF, D_MODEL), single_buffer=True)]
    args = [h, p['norm_ffn_g'], p['w_up'], p['w_ffn_conv'], p['w_down']]
    if state is not None:
        in_specs.append(pl.BlockSpec((None, nb, FFN_CONV - 1, 2 * D_FF), lambda b, i: (l, b, 0, 0)))
        args.append(state)
    return pl.pallas_call(
        functools.partial(_ffn_kernel, nb=nb, tm=tm, n_sub=n_sub, has_state=state is not None),
        grid=(B // nb, T // tm),
        in_specs=in_specs,
        out_specs=[tok, pl.BlockSpec((nb, FFN_CONV - 1, 2 * D_FF), lambda b, i: (b, 0, 0))],
        out_shape=[jax.ShapeDtypeStruct(h.shape, F32),
                   jax.ShapeDtypeStruct((B, FFN_CONV - 1, 2 * D_FF), F32)],
        scratch_shapes=[pltpu.VMEM((nb, SUBLANES, 2 * D_FF), F32)],
        compiler_params=_params(("arbitrary", "arbitrary")),
        name="ffn",
    )(*args)


def _layer(x, l, p, mk, mv, lm, gdn_states, ffn_state, swa_cache, keep):
    B, T, _ = x.shape
    qkv, z, swa, sc, ab = _in_proj(x, l, p)
    oa, oc, s_new, sc_new = _gdn(qkv, ab, z, sc, l, p, gdn_states)
    if swa_cache is None:
        ob, kn = _swa_prompt(swa, l, p)
    else:
        ob, kn = _swa_sample(swa, swa_cache[0], swa_cache[1], l, p)
    h = _post(x, oa, ob, oc, mk, mv, lm, l, p)
    h, f_new = _ffn(h, l, p, ffn_state)
    gb_new = qkv[:, :, T - (GDN_CONV - 1):, :].transpose(0, 2, 1, 3).reshape(
        B, GDN_CONV - 1, GDN_CONV_CH)
    sk = kn[:, T - keep:].reshape(B, keep, SWA_KV_HEADS, SWA_HD)
    sv = swa[:, T - keep:, SWA_Q + SWA_KV:].reshape(B, keep, SWA_KV_HEADS, SWA_HD)
    return h, s_new, gb_new, sc_new, f_new, sk, sv


def kernel(x_prompt, x_sample, mem_prompt, state_gdn, state_gdn_conv, cache_swa_k, cache_swa_v, state_sc_conv, cache_mem_k, cache_mem_v, state_ffn_conv, norm_mix_g, w_in, w_gdn_conv, gdn_a_log, gdn_dt_bias, gdn_norm_g, swa_q_norm_g, swa_k_norm_g, swa_sinks, w_sc_conv, w_o, norm_mem_g, mem_in_norm_g, w_mq, w_mk, w_mv, mem_q_norm_g, mem_k_norm_g, w_mo, norm_ffn_g, w_up, w_ffn_conv, w_down):
    depth = w_in.shape[0]
    Bp, Tp, _ = x_prompt.shape
    Bs, Ts, _ = x_sample.shape
    keep = min(WINDOW, Tp)
    past = cache_swa_k.shape[2]

    hp = jnp.zeros((depth, SUBLANES, LANES), F32)
    hp = hp.at[:, 0, :GDN_HEADS].set(gdn_a_log).at[:, 1, :GDN_HEADS].set(gdn_dt_bias)
    sinks = jnp.zeros((depth, 1, LANES), F32).at[:, 0, :SWA_HEADS].set(swa_sinks)
    row = lambda a: a[:, None, :]
    p = {
        'norm_mix_g': row(norm_mix_g), 'w_in': w_in.astype(BF16), 'w_gdn_conv': w_gdn_conv, 'hp': hp,
        'gdn_norm_g': row(gdn_norm_g), 'swa_q_norm_g': row(swa_q_norm_g),
        'swa_k_norm_g': row(swa_k_norm_g), 'swa_sinks': sinks,
        'swa_q_norm_g_tiled': row(jnp.tile(swa_q_norm_g, (1, SWA_HEADS))),
        'swa_k_norm_g_tiled': row(jnp.tile(swa_k_norm_g, (1, SWA_KV_HEADS))), 'w_sc_conv': w_sc_conv,
        'w_o': w_o.astype(BF16), 'norm_mem_g': row(norm_mem_g), 'mem_in_norm_g': row(mem_in_norm_g),
        'w_mq': w_mq.astype(BF16), 'w_mk': w_mk.astype(BF16), 'w_mv': w_mv.astype(BF16),
        'mem_q_norm_g': row(mem_q_norm_g), 'mem_k_norm_g': row(mem_k_norm_g),
        'w_mo': w_mo.astype(BF16), 'norm_ffn_g': row(norm_ffn_g), 'w_up': w_up.astype(BF16),
        'w_ffn_conv': w_ffn_conv, 'w_down': w_down.astype(BF16),
    }
    cache_k = cache_swa_k.reshape(depth, Bs, past, SWA_KV)
    cache_v = cache_swa_v.reshape(depth, Bs, past, SWA_KV)
    mem_k_s = cache_mem_k.reshape(depth, Bs, MEM_TOKENS, D_MODEL)
    mem_v_s = cache_mem_v.reshape(depth, Bs, MEM_TOKENS, D_MODEL)

    hp_, hs_ = x_prompt, x_sample
    outs_p = {k: [] for k in ('S', 'gb', 'sk', 'sv', 'sb', 'mk', 'mv', 'fb')}
    outs_s = {k: [] for k in ('S', 'gb', 'sk', 'sv', 'sb', 'fb')}
    for l in range(depth):
        mk, mv = _mem_kv(mem_prompt, l, p)
        hp_, S, gb, sb, fb, sk, sv = _layer(hp_, l, p, mk[None], mv[None], 0, None, None, None, keep)
        outs_p['S'].append(S); outs_p['gb'].append(gb); outs_p['sb'].append(sb); outs_p['fb'].append(fb)
        outs_p['sk'].append(sk); outs_p['sv'].append(sv)
        outs_p['mk'].append(mk.reshape(Bp, MEM_TOKENS, MEM_HEADS, MEM_HD))
        outs_p['mv'].append(mv.reshape(Bp, MEM_TOKENS, MEM_HEADS, MEM_HD))
        hs_, S, gb, sb, fb, sk, sv = _layer(
            hs_, l, p, mem_k_s, mem_v_s, l, (state_gdn, state_gdn_conv, state_sc_conv),
            state_ffn_conv, (cache_k, cache_v), Ts)
        outs_s['S'].append(S); outs_s['gb'].append(gb); outs_s['sb'].append(sb); outs_s['fb'].append(fb)
        outs_s['sk'].append(sk); outs_s['sv'].append(sv)
    st = jnp.stack
    return (hp_, hs_,
            st(outs_p['S']), st(outs_p['gb']), st(outs_p['sk']), st(outs_p['sv']), st(outs_p['sb']),
            st(outs_p['mk']), st(outs_p['mv']), st(outs_p['fb']),
            st(outs_s['S']), st(outs_s['gb']), st(outs_s['sk']), st(outs_s['sv']), st(outs_s['sb']),
            st(outs_s['fb']))
```

```python
import functools

import jax
import jax.numpy as jnp
from jax import lax
from jax.experimental import pallas as pl
from jax.experimental.pallas import tpu as pltpu

F32 = jnp.float32
BF16 = jnp.bfloat16

D_MODEL = 1024
CHUNK = 64
EPS = 1e-6
NEG_INF = -1e30
GDN_HEADS = 4
GDN_DK = 128
GDN_DV = 128
GDN_CONV = 4
GDN_QK = GDN_HEADS * GDN_DK
GDN_V = GDN_HEADS * GDN_DV
GDN_CONV_CH = 2 * GDN_QK + GDN_V
SWA_HEADS = 4
SWA_KV_HEADS = 2
SWA_GROUP = SWA_HEADS // SWA_KV_HEADS
SWA_HD = 64
WINDOW = 128
WIN_CHUNKS = WINDOW // CHUNK
SWA_Q = SWA_HEADS * SWA_HD
SWA_KV = SWA_KV_HEADS * SWA_HD
SWA_W = SWA_Q + 2 * SWA_KV
assert SWA_GROUP == SWA_KV_HEADS == 2 and SWA_KV == 128
SC_WIDTH = 256
SC_CONV = 3
MEM_TOKENS = 256
MEM_HEADS = 4
MEM_HD = D_MODEL // MEM_HEADS
D_FF = 2816
FFN_CONV = 3
LANES = 128
SUBLANES = 8
AB_W = LANES
SC_W = 3 * SC_WIDTH
IN_COLS = (
    (0, GDN_CONV_CH),
    (GDN_CONV_CH, GDN_CONV_CH + GDN_V),
    (GDN_CONV_CH + GDN_V, GDN_CONV_CH + GDN_V + SWA_W),
    (GDN_CONV_CH + GDN_V + SWA_W, GDN_CONV_CH + GDN_V + SWA_W + SC_W),
    (GDN_CONV_CH + GDN_V + SWA_W + SC_W, GDN_CONV_CH + GDN_V + SWA_W + SC_W + AB_W),
)
D_IN_PAD = IN_COLS[-1][1]
D_IN = D_IN_PAD - AB_W + 2 * GDN_HEADS
FFN_COL_CHUNKS = ((0, 1024), (1024, 2048), (2048, D_FF))
GDN_CHUNKS_PER_STEP = 4
SUB_TILES = 2
VMEM_LIMIT = 56 * 1024 * 1024
TOKEN_TILE = 512
SWA_QUERY_BLOCK = 1024
MAX_ROWS_SMALL_SEQ = 128


def _seq_tiles(B, T):
    if T >= TOKEN_TILE:
        return 1, TOKEN_TILE
    nb = max(1, min(B, MAX_ROWS_SMALL_SEQ // T))
    while B % nb:
        nb -= 1
    return nb, T


def _sub_tiles(B, T):
    nb, tm = _seq_tiles(B, T)
    if nb == 1 and T % (SUB_TILES * tm) == 0:
        return nb, SUB_TILES * tm, SUB_TILES
    return nb, tm, 1


def _dot(a, b):
    return jnp.dot(a.astype(BF16), b.astype(BF16), preferred_element_type=F32)


def _dot_nt(a, b):
    return lax.dot_general(a.astype(BF16), b.astype(BF16), (((1,), (1,)), ((), ())),
                           preferred_element_type=F32)


def _dot_tn(a, b):
    return lax.dot_general(a.astype(BF16), b.astype(BF16), (((0,), (0,)), ((), ())),
                           preferred_element_type=F32)


def _rms(x, g):
    return x * lax.rsqrt(jnp.mean(x * x, axis=-1, keepdims=True) + EPS) * g


def _silu(x):
    return x * jax.nn.sigmoid(x)


def _lspec(l, shape, single_buffer=False):
    nd = len(shape)
    mode = dict(pipeline_mode=pl.Buffered(1)) if single_buffer else {}
    return pl.BlockSpec((None,) + tuple(shape), lambda *_: (l,) + (0,) * nd, **mode)


def _params(sem):
    return pltpu.CompilerParams(dimension_semantics=sem, vmem_limit_bytes=VMEM_LIMIT)


def _tok_spec(nb, tm, w):
    return pl.BlockSpec((nb, tm, w), lambda b, i: (b, i, 0))


def _block_rows(ref, nb, t0=0, ts=None):
    ts = ref.shape[-2] - t0 if ts is None else ts
    if len(ref.shape) == 3:
        seqs = [ref[b, t0:t0 + ts, :] for b in range(nb)]
    else:
        seqs = [jnp.concatenate([ref[b, c, t0:t0 + ts, :] for c in range(ref.shape[1])], axis=-1)
                for b in range(nb)]
    return seqs[0] if nb == 1 else jnp.concatenate(seqs, axis=0)


def _in_proj_kernel(x_ref, g_ref, w_ref, qkv_ref, z_ref, swa_ref, sc_ref, ab_ref, w_scr, *, nb, tm):
    @pl.when((pl.program_id(0) == 0) & (pl.program_id(1) == 0))
    def _regroup():
        qz = GDN_CONV_CH + GDN_V
        n_gate = 2 * GDN_HEADS
        w_scr[:, 0:qz] = w_ref[:, 0:qz]
        w_scr[:, qz:IN_COLS[-1][0]] = w_ref[:, qz + n_gate:D_IN]
        w_scr[:, IN_COLS[-1][0]:D_IN_PAD] = jnp.concatenate(
            [w_ref[:, qz:qz + n_gate], jnp.zeros((D_MODEL, AB_W - n_gate), BF16)], axis=1)

    xn = _rms(_block_rows(x_ref, nb), g_ref[...]).astype(BF16)
    for o_ref, (lo, hi) in zip((qkv_ref, z_ref, swa_ref, sc_ref, ab_ref), IN_COLS):
        res = jnp.dot(xn, w_scr[:, lo:hi], preferred_element_type=F32)
        for b in range(nb):
            rows = res[b * tm:(b + 1) * tm]
            if len(o_ref.shape) == 4:
                for c in range(o_ref.shape[1]):
                    o_ref[b, c] = rows[:, c * LANES:(c + 1) * LANES]
            else:
                o_ref[b] = rows


def _in_proj(xs, l, p):
    B, T, _ = xs.shape
    nb, tm = _seq_tiles(B, T)
    widths = [hi - lo for lo, hi in IN_COLS]
    slabbed = (True, True, False, False, False)
    out_specs, out_shape = [], []
    for wd, sl in zip(widths, slabbed):
        if sl:
            out_specs.append(pl.BlockSpec((nb, wd // LANES, tm, LANES), lambda b, i: (b, 0, i, 0)))
            out_shape.append(jax.ShapeDtypeStruct((B, wd // LANES, T, LANES), F32))
        else:
            out_specs.append(pl.BlockSpec((nb, tm, wd), lambda b, i: (b, i, 0)))
            out_shape.append(jax.ShapeDtypeStruct((B, T, wd), F32))
    return pl.pallas_call(
        functools.partial(_in_proj_kernel, nb=nb, tm=tm),
        grid=(B // nb, T // tm),
        in_specs=[_tok_spec(nb, tm, D_MODEL), _lspec(l, (1, D_MODEL)),
                  _lspec(l, (D_MODEL, D_IN), single_buffer=True)],
        out_specs=out_specs,
        out_shape=out_shape,
        scratch_shapes=[pltpu.VMEM((D_MODEL, D_IN_PAD), BF16)],
        compiler_params=_params(("arbitrary", "arbitrary")),
        name="in_proj",
    )(xs, p['norm_mix_g'], p['w_in'])


def _cumsum_time(groups):
    prefix = [groups[0]]
    for g in groups[1:]:
        prefix.append(prefix[-1] + g)
    sub = lax.broadcasted_iota(jnp.int32, groups[0].shape, 0)
    run = prefix[-1]
    s = 1
    while s < SUBLANES:
        run = run + jnp.where(sub >= s, pltpu.roll(run, s, axis=0), 0.0)
        s *= 2
    before = jnp.where(sub >= 1, pltpu.roll(run, 1, axis=0), 0.0)
    return [p + before for p in prefix]


def _gdn_kernel(*refs, bb, C, nc, has_state):
    qkv_ref, ab_ref, z_ref, sc_ref, wconv_ref, wsc_ref, hp_ref, gn_ref = refs[:8]
    state_refs = refs[8:11] if has_state else ()
    oa_ref, oc_ref, sout_ref, scout_ref, s_scr, xp_scr, scp_scr = refs[8 + len(state_refs):]
    n = pl.program_id(1)
    pad = SUBLANES
    L = C // SUBLANES
    n_slabs = GDN_CONV_CH // LANES

    def time_of(r):
        return (r % SUBLANES) * L + r // SUBLANES

    def token_row(t):
        return (t % L) * SUBLANES + t // L

    @pl.when(n == 0)
    def _init():
        xp_scr[...] = jnp.zeros_like(xp_scr)
        scp_scr[...] = jnp.zeros_like(scp_scr)
        if has_state:
            s0_ref, gbuf_ref, scbuf_ref = state_refs
            s_scr[...] = s0_ref[...]
            for r in range(GDN_CONV - 1):
                row = token_row(C - (GDN_CONV - 1) + r)
                for b in range(bb):
                    for m in range(n_slabs):
                        xp_scr[b, m, row:row + 1, :] = gbuf_ref[b, r:r + 1, m * LANES:(m + 1) * LANES]
            scp_scr[:, pad - (SC_CONV - 1):pad, :] = scbuf_ref[...]
        else:
            s_scr[...] = jnp.zeros_like(s_scr)

    t_row = time_of(lax.broadcasted_iota(jnp.int32, (C, C), 0))
    t_col = time_of(lax.broadcasted_iota(jnp.int32, (C, C), 1))
    incl = t_row >= t_col
    strict = t_row > t_col
    sub = lax.broadcasted_iota(jnp.int32, (SUBLANES, LANES), 0)
    neg_a = -jnp.exp(hp_ref[0:1, :])
    dt_b = hp_ref[1:2, :]
    n_sq = C.bit_length() - 2

    R = nc * C

    def chunk_groups(window):
        return [window(j) for j in range(L)]

    def shifted(cur, prev, k, cache):
        out = []
        for j in range(L):
            d = j - k
            if d >= 0:
                out.append(cur[d])
                continue
            m = (-d + L - 1) // L
            src = d + m * L
            if (src, m) not in cache:
                cache[src, m] = pltpu.roll(
                    jnp.where(sub >= SUBLANES - m, prev[src], cur[src]), m, axis=0)
            out.append(cache[src, m])
        return jnp.concatenate(out, axis=0)

    ys, gcs, gcts, betas = {}, {}, {}, {}
    for b in range(bb):
        for m in range(n_slabs):
            lanes = slice(m * LANES, (m + 1) * LANES)
            prev = [xp_scr[b, m, j * SUBLANES:(j + 1) * SUBLANES, :] for j in range(L)]
            for c in range(nc):
                cur = chunk_groups(
                    lambda j: qkv_ref[b, m, pl.ds(c * C + j, SUBLANES, stride=L), :])
                cache = {}
                y = shifted(cur, prev, 3, cache) * wconv_ref[0:1, lanes]
                y = y + shifted(cur, prev, 2, cache) * wconv_ref[1:2, lanes]
                y = y + shifted(cur, prev, 1, cache) * wconv_ref[2:3, lanes]
                y = y + jnp.concatenate(cur, axis=0) * wconv_ref[3:4, lanes]
                ys[b, c, m] = _silu(y)
                prev = cur
            xp_scr[b, m] = jnp.concatenate(prev, axis=0)

        for c in range(nc):
            ab = jnp.concatenate(chunk_groups(
                lambda j: ab_ref[b, pl.ds(c * C + j, SUBLANES, stride=L), :]), axis=0)
            t = ab + dt_b
            g = neg_a * (jnp.maximum(t, 0.0) + jnp.log1p(jnp.exp(-jnp.abs(t))))
            betas[b, c] = jax.nn.sigmoid(ab)
            gc = jnp.concatenate(_cumsum_time(
                [g[j * SUBLANES:(j + 1) * SUBLANES] for j in range(L)]), axis=0)
            gcs[b, c] = gc
            gcts[b, c] = jnp.transpose(jnp.concatenate(
                [gc, jnp.zeros((LANES - C, LANES), F32)], axis=0))

        scv = sc_ref[b]
        prod = scv[:, SC_WIDTH:2 * SC_WIDTH] * scv[:, 2 * SC_WIDTH:3 * SC_WIDTH]
        pext = jnp.concatenate([scp_scr[b], prod], axis=0)
        cu = pltpu.roll(pext, 2, axis=0)[pad:] * wsc_ref[0:1, :]
        cu = cu + pltpu.roll(pext, 1, axis=0)[pad:] * wsc_ref[1:2, :]
        cu = cu + prod * wsc_ref[2:3, :]
        scp_scr[b] = prod[R - pad:R, :]
        oc_ref[b] = scv[:, 0:SC_WIDTH] * cu

    problems = [(c, b, h) for c in range(nc) for b in range(bb) for h in range(GDN_HEADS)]
    every = range(len(problems))
    qs, ks, vs, kbs, bts, egs, e_incls, e_stricts, kds, gls = ([] for _ in range(10))
    for c, b, h in problems:
        gc = gcs[b, c]
        q = ys[b, c, h]
        k = ys[b, c, GDN_HEADS + h]
        q = q * lax.rsqrt(jnp.sum(q * q, axis=-1, keepdims=True) + EPS) * (GDN_DK ** -0.5)
        k = k * lax.rsqrt(jnp.sum(k * k, axis=-1, keepdims=True) + EPS)
        gci = gc[:, h:h + 1]
        gcj = gcts[b, c][h:h + 1, 0:C]
        g_last = gc[C - 1:C, h:h + 1]
        e_incl = jnp.exp(jnp.where(incl, gci - gcj, NEG_INF))
        qs.append(q)
        ks.append(k)
        kbs.append(k.astype(BF16))
        vs.append(ys[b, c, 2 * GDN_HEADS + h])
        bts.append(betas[b, c][:, GDN_HEADS + h:GDN_HEADS + h + 1])
        egs.append(jnp.exp(gci))
        e_incls.append(e_incl)
        e_stricts.append(jnp.where(strict, e_incl, 0.0))
        kds.append(k * jnp.exp(g_last - gci))
        gls.append(jnp.exp(g_last))

    kks = [_dot_nt(kbs[i], kbs[i]) for i in every]
    qks = [_dot_nt(qs[i], kbs[i]) * e_incls[i] for i in every]
    p_pows = [-(bts[i] * kks[i] * e_stricts[i]) for i in every]
    n_accs = list(p_pows)
    for _ in range(n_sq):
        pbs = [p.astype(BF16) for p in p_pows]
        p_pows = [_dot(pb, pb) for pb in pbs]
        n_accs = [n_accs[i] + p_pows[i] + _dot(p_pows[i], n_accs[i]) for i in every]
    rhss = [jnp.concatenate([bts[i] * vs[i], (bts[i] * egs[i]) * ks[i]], axis=-1) for i in every]
    sols = [rhss[i] + _dot(n_accs[i], rhss[i]) for i in every]

    pairs = [(b, h) for b in range(bb) for h in range(GDN_HEADS)]
    here = range(len(pairs))
    states = [s_scr[b, h] for b, h in pairs]
    for c in range(nc):
        base = c * len(pairs)
        sbs = [s.astype(BF16) for s in states]
        o_inter = [_dot(qs[base + j] * egs[base + j], sbs[j]) for j in here]
        us = [sols[base + j][:, :GDN_DV] - _dot(sols[base + j][:, GDN_DV:], sbs[j]) for j in here]
        outs = [o_inter[j] + _dot(qks[base + j], us[j]) for j in here]
        states = [gls[base + j] * states[j] + _dot_tn(kds[base + j], us[j]) for j in here]
        for j, (b, h) in enumerate(pairs):
            zh = jnp.concatenate(chunk_groups(
                lambda i: z_ref[b, h, pl.ds(c * C + i, SUBLANES, stride=L), :]), axis=0)
            res = _rms(outs[j], gn_ref[...]) * _silu(zh)
            for i in range(L):
                oa_ref[b, h, pl.ds(c * C + i, SUBLANES, stride=L), :] = res[
                    i * SUBLANES:(i + 1) * SUBLANES]
    for j, (b, h) in enumerate(pairs):
        s_scr[b, h] = states[j]

    @pl.when(n == pl.num_programs(1) - 1)
    def _fin():
        sout_ref[...] = s_scr[...]
        scout_ref[...] = scp_scr[:, pad - (SC_CONV - 1):pad, :]


def _gdn(qkv, ab, z, sc, l, p, states):
    B, n_slabs, T, _ = qkv.shape
    C = CHUNK if T % CHUNK == 0 else T
    assert C % SUBLANES == 0
    bb = 2 if B % 2 == 0 else 1
    nc = GDN_CHUNKS_PER_STEP if (T // C) % GDN_CHUNKS_PER_STEP == 0 else 1
    tok = lambda w: pl.BlockSpec((bb, nc * C, w), lambda i, n: (i, n, 0))
    slabs = lambda k: pl.BlockSpec((bb, k, nc * C, LANES), lambda i, n: (i, 0, n, 0))
    per_b = lambda *s: pl.BlockSpec((bb,) + s, lambda i, n: (i,) + (0,) * len(s))
    per_lb = lambda *s: pl.BlockSpec((None, bb) + s, lambda i, n: (l, i) + (0,) * len(s))
    in_specs = [slabs(n_slabs), tok(AB_W), slabs(GDN_HEADS), tok(SC_W),
                _lspec(l, (GDN_CONV, GDN_CONV_CH)), _lspec(l, (SC_CONV, SC_WIDTH)),
                _lspec(l, (SUBLANES, LANES)), _lspec(l, (1, GDN_DV))]
    args = [qkv, ab, z, sc, p['w_gdn_conv'], p['w_sc_conv'], p['hp'], p['gdn_norm_g']]
    if states is not None:
        in_specs += [per_lb(GDN_HEADS, GDN_DK, GDN_DV), per_lb(GDN_CONV - 1, GDN_CONV_CH),
                     per_lb(SC_CONV - 1, SC_WIDTH)]
        args += list(states)
    return pl.pallas_call(
        functools.partial(_gdn_kernel, bb=bb, C=C, nc=nc, has_state=states is not None),
        grid=(B // bb, T // (nc * C)),
        in_specs=in_specs,
        out_specs=[slabs(GDN_HEADS), tok(SC_WIDTH),
                   per_b(GDN_HEADS, GDN_DK, GDN_DV), per_b(SC_CONV - 1, SC_WIDTH)],
        out_shape=[jax.ShapeDtypeStruct((B, GDN_HEADS, T, GDN_DV), F32),
                   jax.ShapeDtypeStruct((B, T, SC_WIDTH), F32),
                   jax.ShapeDtypeStruct((B, GDN_HEADS, GDN_DK, GDN_DV), F32),
                   jax.ShapeDtypeStruct((B, SC_CONV - 1, SC_WIDTH), F32)],
        scratch_shapes=[pltpu.VMEM((bb, GDN_HEADS, GDN_DK, GDN_DV), F32),
                        pltpu.VMEM((bb, n_slabs, C, LANES), F32),
                        pltpu.VMEM((bb, SUBLANES, SC_WIDTH), F32)],
        compiler_params=_params(("arbitrary", "arbitrary")),
        name="gdn",
    )(*args)


def _swa_norm_k(x, gk):
    return jnp.concatenate(
        [_rms(x[:, SWA_Q + i * SWA_HD:SWA_Q + (i + 1) * SWA_HD], gk) for i in range(SWA_KV_HEADS)],
        axis=-1)


def _swa_norm_q(x, gq):
    return [_rms(x[:, h * SWA_HD:(h + 1) * SWA_HD], gq) for h in range(SWA_HEADS)]


def _group_queries(qn, g, r0, r1, sinks):
    heads = range(g * SWA_GROUP, (g + 1) * SWA_GROUP)
    q = jnp.concatenate([qn[h][r0:r1] for h in heads], axis=0)
    sink = jnp.concatenate(
        [jnp.broadcast_to(sinks[:, h:h + 1], (r1 - r0, 1)) for h in heads], axis=0)
    return q, sink


def _attend_many(problems):
    scores = [_dot_nt(q, k) * (SWA_HD ** -0.5) for q, k, _, _, _ in problems]
    probs = []
    for s, (_, _, _, valid, sink) in zip(scores, problems):
        if valid is not None:
            s = jnp.where(valid, s, NEG_INF)
        m = jnp.maximum(jnp.max(s, axis=-1, keepdims=True), sink)
        p = jnp.exp(s - m)
        probs.append(p / (jnp.sum(p, axis=-1, keepdims=True) + jnp.exp(sink - m)))
    return [_dot(p, pr[2]) for p, pr in zip(probs, problems)]


def _ungroup(outs, rows):
    return jnp.concatenate(
        [o[j * rows:(j + 1) * rows] for o in outs for j in range(SWA_GROUP)], axis=-1)


def _split_bf16(x):
    hi = x.astype(BF16)
    return hi, (x - hi.astype(F32)).astype(BF16)


def _sum_dot(x, ones):
    hi, lo = _split_bf16(x)
    return (jnp.dot(hi, ones, preferred_element_type=F32)
            + jnp.dot(lo, ones, preferred_element_type=F32))


def _head_rms(x, g_tiled):
    w = x.shape[1]
    same_head = (lax.broadcasted_iota(jnp.int32, (w, w), 0) // SWA_HD
                 == lax.broadcasted_iota(jnp.int32, (w, w), 1) // SWA_HD)
    ss = _sum_dot(x * x, jnp.where(same_head, 1.0, 0.0).astype(BF16))
    return x * lax.rsqrt(ss * (1.0 / SWA_HD) + EPS) * g_tiled


def _swa_prompt_kernel(x_ref, halo_ref, gq_ref, gk_ref, sink_ref, o_ref, kn_ref, *, qb):
    i = pl.program_id(1)
    band = (WIN_CHUNKS + 1) * CHUNK
    n_chunks = qb // CHUNK
    x = x_ref[...]
    hx = halo_ref[...]
    kcols = slice(SWA_Q, SWA_Q + SWA_KV)
    vcols = slice(SWA_Q + SWA_KV, SWA_W)
    qn = _head_rms(x[:, 0:SWA_Q], gq_ref[...])
    k_all = _head_rms(jnp.concatenate([hx[:, kcols], x[:, kcols]], axis=0), gk_ref[...])
    v_all = jnp.concatenate([hx[:, vcols], x[:, vcols]], axis=0)
    kn_ref[...] = k_all[WINDOW:]

    lane_half = lax.broadcasted_iota(jnp.int32, (1, SWA_KV), 1) // SWA_HD

    def variants(a):
        rot = pltpu.roll(a, SWA_HD, axis=1)
        return [[jnp.where(lane_half == j, a if j == g else rot, 0.0).astype(BF16)
                 for j in range(SWA_GROUP)] for g in range(SWA_KV_HEADS)]

    k_var, v_var = variants(k_all), variants(v_all)
    ones = jnp.ones((band, SWA_KV), BF16)
    local_chunk = lax.broadcasted_iota(jnp.int32, (1, band), 1) // CHUNK
    problems = [(c, g, j) for c in range(n_chunks) for g in range(SWA_KV_HEADS)
                for j in range(SWA_GROUP)]
    scores = []
    for c, g, j in problems:
        q = qn[c * CHUNK:(c + 1) * CHUNK, g * SWA_KV:(g + 1) * SWA_KV]
        scores.append(_dot_nt(q, k_var[g][j][c * CHUNK:c * CHUNK + band]) * (SWA_HD ** -0.5))
    ps, tails = [], []
    for s, (c, g, j) in zip(scores, problems):
        valid = local_chunk + (i * n_chunks + c - WIN_CHUNKS) >= 0
        s = jnp.where(valid, s, NEG_INF)
        sink = sink_ref[:, g * SWA_GROUP + j:g * SWA_GROUP + j + 1]
        m = jnp.maximum(jnp.max(s, axis=-1, keepdims=True), sink)
        ps.append(jnp.exp(s - m))
        tails.append(jnp.exp(sink - m))
    outs = []
    for p, tail, (c, g, j) in zip(ps, tails, problems):
        denom = _sum_dot(p, ones) + tail
        outs.append(_dot(p, v_var[g][j][c * CHUNK:c * CHUNK + band]) / denom)
    for c in range(n_chunks):
        for g in range(SWA_KV_HEADS):
            k0 = (c * SWA_KV_HEADS + g) * SWA_GROUP
            o_ref[c * CHUNK:(c + 1) * CHUNK, g * SWA_KV:(g + 1) * SWA_KV] = sum(
                outs[k0 + 1:k0 + SWA_GROUP], outs[k0])


def _swa_prompt(swa, l, p):
    B, T, _ = swa.shape
    qb = min(SWA_QUERY_BLOCK, T)
    assert T % qb == 0 and qb % WINDOW == 0
    halo_per_block = qb // WINDOW
    return pl.pallas_call(
        functools.partial(_swa_prompt_kernel, qb=qb),
        grid=(B, T // qb),
        in_specs=[pl.BlockSpec((None, qb, SWA_W), lambda b, i: (b, i, 0)),
                  pl.BlockSpec((None, WINDOW, SWA_W),
                               lambda b, i: (b, jnp.maximum(i * halo_per_block - 1, 0), 0)),
                  _lspec(l, (1, SWA_Q)), _lspec(l, (1, SWA_KV)), _lspec(l, (1, LANES))],
        out_specs=[pl.BlockSpec((None, qb, SWA_Q), lambda b, i: (b, i, 0)),
                   pl.BlockSpec((None, qb, SWA_KV), lambda b, i: (b, i, 0))],
        out_shape=[jax.ShapeDtypeStruct((B, T, SWA_Q), F32),
                   jax.ShapeDtypeStruct((B, T, SWA_KV), F32)],
        compiler_params=_params(("arbitrary", "arbitrary")),
        name="swa_prompt",
    )(swa, swa, p['swa_q_norm_g_tiled'], p['swa_k_norm_g_tiled'], p['swa_sinks'])


def _swa_sample_kernel(x_ref, kp_ref, vp_ref, gq_ref, gk_ref, sink_ref, o_ref, kn_ref, *, bb, T):
    problems = []
    for b in range(bb):
        x = x_ref[b]
        kn = _swa_norm_k(x, gk_ref[...])
        kn_ref[b] = kn
        k_all = jnp.concatenate([kp_ref[b], kn], axis=0)
        v_all = jnp.concatenate([vp_ref[b], x[:, SWA_Q + SWA_KV:SWA_W]], axis=0)
        qn = _swa_norm_q(x, gq_ref[...])
        for g in range(SWA_KV_HEADS):
            q, sink = _group_queries(qn, g, 0, T, sink_ref[...])
            cols = slice(g * SWA_HD, (g + 1) * SWA_HD)
            problems.append((q, k_all[:, cols], v_all[:, cols], None, sink))
    outs = _attend_many(problems)
    for b in range(bb):
        o_ref[b] = _ungroup(outs[b * SWA_KV_HEADS:(b + 1) * SWA_KV_HEADS], T)


def _swa_sample(swa, k_past, v_past, l, p):
    B, T, _ = swa.shape
    past = k_past.shape[2]
    bb = 4 if B % 4 == 0 else 1
    blk = lambda r, w: pl.BlockSpec((bb, r, w), lambda b: (b, 0, 0))
    cache = pl.BlockSpec((None, bb, past, SWA_KV), lambda b: (l, b, 0, 0))
    return pl.pallas_call(
        functools.partial(_swa_sample_kernel, bb=bb, T=T),
        grid=(B // bb,),
        in_specs=[blk(T, SWA_W), cache, cache,
                  _lspec(l, (1, SWA_HD)), _lspec(l, (1, SWA_HD)), _lspec(l, (1, LANES))],
        out_specs=[blk(T, SWA_Q), blk(T, SWA_KV)],
        out_shape=[jax.ShapeDtypeStruct((B, T, SWA_Q), F32),
                   jax.ShapeDtypeStruct((B, T, SWA_KV), F32)],
        compiler_params=_params(("arbitrary",)),
        name="swa_sample",
    )(swa, k_past, v_past, p['swa_q_norm_g'], p['swa_k_norm_g'], p['swa_sinks'])


def _mem_kv_kernel(m_ref, g_ref, wk_ref, wv_ref, gk_ref, k_ref, v_ref):
    m = _rms(m_ref[...], g_ref[...]).astype(BF16)
    k = jnp.dot(m, wk_ref[...], preferred_element_type=F32)
    k_ref[...] = jnp.concatenate(
        [_rms(k[:, h * MEM_HD:(h + 1) * MEM_HD], gk_ref[...]) for h in range(MEM_HEADS)], axis=-1)
    v_ref[...] = jnp.dot(m, wv_ref[...], preferred_element_type=F32)


def _mem_kv(mem, p):
    B, M, _ = mem.shape
    depth = p['w_mk'].shape[0]
    per_layer = lambda *s: pl.BlockSpec((None,) + s, lambda l, b: (l,) + (0,) * len(s))
    out = pl.BlockSpec((None, None, M, D_MODEL), lambda l, b: (l, b, 0, 0))
    return pl.pallas_call(
        _mem_kv_kernel,
        grid=(depth, B),
        in_specs=[pl.BlockSpec((None, M, D_MODEL), lambda l, b: (b, 0, 0)),
                  per_layer(1, D_MODEL), per_layer(D_MODEL, D_MODEL),
                  per_layer(D_MODEL, D_MODEL), per_layer(1, MEM_HD)],
        out_specs=[out, out],
        out_shape=[jax.ShapeDtypeStruct((depth, B, M, D_MODEL), F32)] * 2,
        compiler_params=_params(("arbitrary", "arbitrary")),
        name="mem_kv",
    )(mem, p['mem_in_norm_g'], p['w_mk'], p['w_mv'], p['mem_k_norm_g'])


def _post_kernel(x_ref, a_ref, b_ref, c_ref, wo_ref, gm_ref, wq_ref, gq_ref, mk_ref, mv_ref,
                 wmo_ref, o_ref, *, nb, tm, n_sub):
    ts = tm // n_sub
    probs = [(b, slice(hd * MEM_HD, (hd + 1) * MEM_HD)) for b in range(nb) for hd in range(MEM_HEADS)]
    hs, scores = [], []
    for sub in range(n_sub):
        rows = lambda ref: _block_rows(ref, nb, sub * ts, ts)
        h = rows(x_ref)
        h = h + (jnp.dot(rows(a_ref).astype(BF16), wo_ref[0:GDN_V, :], preferred_element_type=F32)
                 + jnp.dot(rows(b_ref).astype(BF16), wo_ref[GDN_V:GDN_V + SWA_Q, :],
                           preferred_element_type=F32)
                 + jnp.dot(rows(c_ref).astype(BF16), wo_ref[GDN_V + SWA_Q:, :],
                           preferred_element_type=F32))
        hs.append(h)
        hn = _rms(h, gm_ref[...]).astype(BF16)
        q = jnp.dot(hn, wq_ref[...], preferred_element_type=F32)
        qhs = [_rms(q[b * ts:(b + 1) * ts, sl], gq_ref[...]) for b, sl in probs]
        scores.append([_dot_nt(qh, mk_ref[b, :, sl]) * (MEM_HD ** -0.5)
                       for qh, (b, sl) in zip(qhs, probs)])
    ps = []
    for sub in range(n_sub):
        ps.append([])
        for s in scores[sub]:
            p = jnp.exp(s - jnp.max(s, axis=-1, keepdims=True))
            ps[sub].append(p / jnp.sum(p, axis=-1, keepdims=True))
    for sub in range(n_sub):
        outs = [_dot(p, mv_ref[b, :, sl]) for p, (b, sl) in zip(ps[sub], probs)]
        o = jnp.concatenate(
            [jnp.concatenate(outs[b * MEM_HEADS:(b + 1) * MEM_HEADS], axis=-1) for b in range(nb)],
            axis=0).astype(BF16)
        res = hs[sub] + jnp.dot(o, wmo_ref[...], preferred_element_type=F32)
        for b in range(nb):
            o_ref[b, sub * ts:(sub + 1) * ts, :] = res[b * ts:(b + 1) * ts]


def _post(xs, oa, ob, oc, mk, mv, lm, l, p):
    B, T, _ = xs.shape
    nb, tm, n_sub = _sub_tiles(B, T)
    tok = lambda w: _tok_spec(nb, tm, w)
    mem = pl.BlockSpec((None, nb, MEM_TOKENS, D_MODEL), lambda b, i: (lm, b, 0, 0))
    sq = _lspec(l, (D_MODEL, D_MODEL), single_buffer=True)
    return pl.pallas_call(
        functools.partial(_post_kernel, nb=nb, tm=tm, n_sub=n_sub),
        grid=(B // nb, T // tm),
        in_specs=[tok(D_MODEL),
                  pl.BlockSpec((nb, GDN_HEADS, tm, GDN_DV), lambda b, i: (b, 0, i, 0)),
                  tok(SWA_Q), tok(SC_WIDTH), sq,
                  _lspec(l, (1, D_MODEL)), sq, _lspec(l, (1, MEM_HD)), mem, mem, sq],
        out_specs=tok(D_MODEL),
        out_shape=jax.ShapeDtypeStruct((B, T, D_MODEL), F32),
        compiler_params=_params(("arbitrary", "arbitrary")),
        name="post",
    )(xs, oa, ob, oc, p['w_o'], p['norm_mem_g'], p['w_mq'], p['mem_q_norm_g'], mk, mv, p['w_mo'])


def _ffn_kernel(*refs, nb, tm, n_sub, has_state):
    h_ref, g_ref, wup_ref, wc_ref, wdn_ref = refs[:5]
    buf_ref = refs[5] if has_state else None
    o_ref, st_ref, carry_scr = refs[5 + int(has_state):]
    i = pl.program_id(1)
    ts = tm // n_sub

    @pl.when(i == 0)
    def _init():
        carry_scr[...] = jnp.zeros_like(carry_scr)
        if has_state:
            for b in range(nb):
                u_m2, u_m1 = buf_ref[b, 0:1, :], buf_ref[b, 1:2, :]
                carry_scr[b, 0:1, :] = u_m1 * wc_ref[0:1, :]
                carry_scr[b, 1:2, :] = u_m1 * wc_ref[1:2, :] + u_m2 * wc_ref[0:1, :]
                carry_scr[b, 2:4, :] = buf_ref[b]

    def delayed(x, carry_row, cols):
        rolled = pltpu.roll(x, 1, axis=0)
        first_row = lax.broadcasted_iota(jnp.int32, (SUBLANES, x.shape[1]), 0) == 0
        pieces = []
        for b in range(nb):
            r0 = b * ts
            pieces.append(jnp.where(first_row, carry_scr[b, carry_row:carry_row + 1, cols],
                                    rolled[r0:r0 + SUBLANES]))
            pieces.append(rolled[r0 + SUBLANES:r0 + ts])
        return jnp.concatenate(pieces, axis=0)

    def conv_act(xn, lo, hi):
        conv = []
        for base in (0, D_FF):
            cols = slice(base + lo, base + hi)
            u = jnp.dot(xn, wup_ref[:, cols], preferred_element_type=F32)
            first = u * wc_ref[0:1, cols]
            mid = u * wc_ref[1:2, cols] + delayed(first, 0, cols)
            y = delayed(mid, 1, cols) + u * wc_ref[2:3, cols]
            for b in range(nb):
                last = (b + 1) * ts - 1
                carry_scr[b, 0:1, cols] = first[last:last + 1]
                carry_scr[b, 1:2, cols] = mid[last:last + 1]
                carry_scr[b, 2:4, cols] = u[last - 1:last + 1]
            conv.append(y)
        return (_silu(conv[0]) * conv[1]).astype(BF16)

    def down(acc, act, lo, hi):
        return acc + jnp.dot(act, wdn_ref[lo:hi, :], preferred_element_type=F32)

    n_ch = len(FFN_COL_CHUNKS)
    for sub in range(n_sub):
        t0 = sub * ts
        hs = _block_rows(h_ref, nb, t0, ts)
        xn = _rms(hs, g_ref[...]).astype(BF16)
        acc = hs
        act = conv_act(xn, *FFN_COL_CHUNKS[0])
        for k in range(n_ch):
            nxt = conv_act(xn, *FFN_COL_CHUNKS[k + 1]) if k + 1 < n_ch else None
            acc = down(acc, act, *FFN_COL_CHUNKS[k])
            act = nxt
        for b in range(nb):
            o_ref[b, t0:t0 + ts, :] = acc[b * ts:(b + 1) * ts]

    @pl.when(i == pl.num_programs(1) - 1)
    def _fin():
        for b in range(nb):
            st_ref[b] = carry_scr[b, 2:4, :]


def _ffn(h, l, p, state):
    B, T, _ = h.shape
    nb, tm, n_sub = _sub_tiles(B, T)
    assert (tm // n_sub) % (2 * SUBLANES) == 0
    tok = _tok_spec(nb, tm, D_MODEL)
    in_specs = [tok, _lspec(l, (1, D_MODEL)), _lspec(l, (D_MODEL, 2 * D_FF), single_buffer=True),
                _lspec(l, (FFN_CONV, 2 * D_FF)), _lspec(l, (D_FF, D_MODEL), single_buffer=True)]
    args = [h, p['norm_ffn_g'], p['w_up'], p['w_ffn_conv'], p['w_down']]
    if state is not None:
        in_specs.append(pl.BlockSpec((None, nb, FFN_CONV - 1, 2 * D_FF), lambda b, i: (l, b, 0, 0)))
        args.append(state)
    return pl.pallas_call(
        functools.partial(_ffn_kernel, nb=nb, tm=tm, n_sub=n_sub, has_state=state is not None),
        grid=(B // nb, T // tm),
        in_specs=in_specs,
        out_specs=[tok, pl.BlockSpec((nb, FFN_CONV - 1, 2 * D_FF), lambda b, i: (b, 0, 0))],
        out_shape=[jax.ShapeDtypeStruct(h.shape, F32),
                   jax.ShapeDtypeStruct((B, FFN_CONV - 1, 2 * D_FF), F32)],
        scratch_shapes=[pltpu.VMEM((nb, SUBLANES, 2 * D_FF), F32)],
        compiler_params=_params(("arbitrary", "arbitrary")),
        name="ffn",
    )(*args)


def _layer(x, l, p, mk, mv, lm, gdn_states, ffn_state, swa_cache, keep):
    B, T, _ = x.shape
    qkv, z, swa, sc, ab = _in_proj(x, l, p)
    oa, oc, s_new, sc_new = _gdn(qkv, ab, z, sc, l, p, gdn_states)
    if swa_cache is None:
        ob, kn = _swa_prompt(swa, l, p)
    else:
        ob, kn = _swa_sample(swa, swa_cache[0], swa_cache[1], l, p)
    h = _post(x, oa, ob, oc, mk, mv, lm, l, p)
    h, f_new = _ffn(h, l, p, ffn_state)
    gb_new = qkv[:, :, T - (GDN_CONV - 1):, :].transpose(0, 2, 1, 3).reshape(
        B, GDN_CONV - 1, GDN_CONV_CH)
    sk = kn[:, T - keep:].reshape(B, keep, SWA_KV_HEADS, SWA_HD)
    sv = swa[:, T - keep:, SWA_Q + SWA_KV:].reshape(B, keep, SWA_KV_HEADS, SWA_HD)
    return h, s_new, gb_new, sc_new, f_new, sk, sv


def kernel(x_prompt, x_sample, mem_prompt, state_gdn, state_gdn_conv, cache_swa_k, cache_swa_v, state_sc_conv, cache_mem_k, cache_mem_v, state_ffn_conv, norm_mix_g, w_in, w_gdn_conv, gdn_a_log, gdn_dt_bias, gdn_norm_g, swa_q_norm_g, swa_k_norm_g, swa_sinks, w_sc_conv, w_o, norm_mem_g, mem_in_norm_g, w_mq, w_mk, w_mv, mem_q_norm_g, mem_k_norm_g, w_mo, norm_ffn_g, w_up, w_ffn_conv, w_down):
    depth = w_in.shape[0]
    Bp, Tp, _ = x_prompt.shape
    Bs, Ts, _ = x_sample.shape
    keep = min(WINDOW, Tp)
    past = cache_swa_k.shape[2]

    hp = jnp.zeros((depth, SUBLANES, LANES), F32)
    hp = hp.at[:, 0, :GDN_HEADS].set(gdn_a_log).at[:, 1, :GDN_HEADS].set(gdn_dt_bias)
    sinks = jnp.zeros((depth, 1, LANES), F32).at[:, 0, :SWA_HEADS].set(swa_sinks)
    row = lambda a: a[:, None, :]
    p = {
        'norm_mix_g': row(norm_mix_g), 'w_in': w_in.astype(BF16), 'w_gdn_conv': w_gdn_conv, 'hp': hp,
        'gdn_norm_g': row(gdn_norm_g), 'swa_q_norm_g': row(swa_q_norm_g),
        'swa_k_norm_g': row(swa_k_norm_g), 'swa_sinks': sinks,
        'swa_q_norm_g_tiled': row(jnp.tile(swa_q_norm_g, (1, SWA_HEADS))),
        'swa_k_norm_g_tiled': row(jnp.tile(swa_k_norm_g, (1, SWA_KV_HEADS))), 'w_sc_conv': w_sc_conv,
        'w_o': w_o.astype(BF16), 'norm_mem_g': row(norm_mem_g), 'mem_in_norm_g': row(mem_in_norm_g),
        'w_mq': w_mq.astype(BF16), 'w_mk': w_mk.astype(BF16), 'w_mv': w_mv.astype(BF16),
        'mem_q_norm_g': row(mem_q_norm_g), 'mem_k_norm_g': row(mem_k_norm_g),
        'w_mo': w_mo.astype(BF16), 'norm_ffn_g': row(norm_ffn_g), 'w_up': w_up.astype(BF16),
        'w_ffn_conv': w_ffn_conv, 'w_down': w_down.astype(BF16),
    }
    cache_k = cache_swa_k.reshape(depth, Bs, past, SWA_KV)
    cache_v = cache_swa_v.reshape(depth, Bs, past, SWA_KV)
    mem_k_s = cache_mem_k.reshape(depth, Bs, MEM_TOKENS, D_MODEL).astype(BF16)
    mem_v_s = cache_mem_v.reshape(depth, Bs, MEM_TOKENS, D_MODEL).astype(BF16)

    hp_, hs_ = x_prompt, x_sample
    outs_p = {k: [] for k in ('S', 'gb', 'sk', 'sv', 'sb', 'fb')}
    outs_s = {k: [] for k in ('S', 'gb', 'sk', 'sv', 'sb', 'fb')}
    mk, mv = _mem_kv(mem_prompt, p)
    for l in range(depth):
        hp_, S, gb, sb, fb, sk, sv = _layer(hp_, l, p, mk, mv, l, None, None, None, keep)
        outs_p['S'].append(S); outs_p['gb'].append(gb); outs_p['sb'].append(sb); outs_p['fb'].append(fb)
        outs_p['sk'].append(sk); outs_p['sv'].append(sv)
        hs_, S, gb, sb, fb, sk, sv = _layer(
            hs_, l, p, mem_k_s, mem_v_s, l, (state_gdn, state_gdn_conv, state_sc_conv),
            state_ffn_conv, (cache_k, cache_v), Ts)
        outs_s['S'].append(S); outs_s['gb'].append(gb); outs_s['sb'].append(sb); outs_s['fb'].append(fb)
        outs_s['sk'].append(sk); outs_s['sv'].append(sv)
    st = jnp.stack
    return (hp_, hs_,
            st(outs_p['S']), st(outs_p['gb']), st(outs_p['sk']), st(outs_p['sv']), st(outs_p['sb']),
            mk.reshape(depth, Bp, MEM_TOKENS, MEM_HEADS, MEM_HD),
            mv.reshape(depth, Bp, MEM_TOKENS, MEM_HEADS, MEM_HD), st(outs_p['fb']),
            st(outs_s['S']), st(outs_s['gb']), st(outs_s['sk']), st(outs_s['sv']), st(outs_s['sb']),
            st(outs_s['fb']))
```

```python
import functools

import jax
import jax.numpy as jnp
from jax import lax
from jax.experimental import pallas as pl
from jax.experimental.pallas import tpu as pltpu

F32 = jnp.float32
BF16 = jnp.bfloat16

D_MODEL = 1024
CHUNK = 64
EPS = 1e-6
NEG_INF = -1e30
GDN_HEADS = 4
GDN_DK = 128
GDN_DV = 128
GDN_CONV = 4
GDN_QK = GDN_HEADS * GDN_DK
GDN_V = GDN_HEADS * GDN_DV
GDN_CONV_CH = 2 * GDN_QK + GDN_V
SWA_HEADS = 4
SWA_KV_HEADS = 2
SWA_GROUP = SWA_HEADS // SWA_KV_HEADS
SWA_HD = 64
WINDOW = 128
WIN_CHUNKS = WINDOW // CHUNK
SWA_Q = SWA_HEADS * SWA_HD
SWA_KV = SWA_KV_HEADS * SWA_HD
SWA_W = SWA_Q + 2 * SWA_KV
assert SWA_GROUP == SWA_KV_HEADS == 2 and SWA_KV == 128
SC_WIDTH = 256
SC_CONV = 3
MEM_TOKENS = 256
MEM_HEADS = 4
MEM_HD = D_MODEL // MEM_HEADS
D_FF = 2816
FFN_CONV = 3
LANES = 128
SUBLANES = 8
AB_W = LANES
SC_W = 3 * SC_WIDTH
IN_COLS = (
    (0, GDN_CONV_CH),
    (GDN_CONV_CH, GDN_CONV_CH + GDN_V),
    (GDN_CONV_CH + GDN_V, GDN_CONV_CH + GDN_V + SWA_W),
    (GDN_CONV_CH + GDN_V + SWA_W, GDN_CONV_CH + GDN_V + SWA_W + SC_W),
    (GDN_CONV_CH + GDN_V + SWA_W + SC_W, GDN_CONV_CH + GDN_V + SWA_W + SC_W + AB_W),
)
D_IN_PAD = IN_COLS[-1][1]
D_IN = D_IN_PAD - AB_W + 2 * GDN_HEADS
FFN_COL_CHUNKS = ((0, 1024), (1024, 2048), (2048, D_FF))
GDN_CHUNKS_PER_STEP = 4
SUB_TILES = 2
VMEM_LIMIT = 56 * 1024 * 1024
TOKEN_TILE = 512
SWA_QUERY_BLOCK = 1024
MAX_ROWS_SMALL_SEQ = 128


def _seq_tiles(B, T):
    if T >= TOKEN_TILE:
        return 1, TOKEN_TILE
    nb = max(1, min(B, MAX_ROWS_SMALL_SEQ // T))
    while B % nb:
        nb -= 1
    return nb, T


def _sub_tiles(B, T):
    nb, tm = _seq_tiles(B, T)
    if nb == 1 and T % (SUB_TILES * tm) == 0:
        return nb, SUB_TILES * tm, SUB_TILES
    return nb, tm, 1


def _dot(a, b):
    return jnp.dot(a.astype(BF16), b.astype(BF16), preferred_element_type=F32)


def _dot_nt(a, b):
    return lax.dot_general(a.astype(BF16), b.astype(BF16), (((1,), (1,)), ((), ())),
                           preferred_element_type=F32)


def _dot_tn(a, b):
    return lax.dot_general(a.astype(BF16), b.astype(BF16), (((0,), (0,)), ((), ())),
                           preferred_element_type=F32)


def _rms(x, g):
    return x * lax.rsqrt(jnp.mean(x * x, axis=-1, keepdims=True) + EPS) * g


def _silu(x):
    return x * jax.nn.sigmoid(x)


def _lspec(l, shape, single_buffer=False):
    nd = len(shape)
    mode = dict(pipeline_mode=pl.Buffered(1)) if single_buffer else {}
    return pl.BlockSpec((None,) + tuple(shape), lambda *_: (l,) + (0,) * nd, **mode)


def _params(sem):
    return pltpu.CompilerParams(dimension_semantics=sem, vmem_limit_bytes=VMEM_LIMIT)


def _tok_spec(nb, tm, w):
    return pl.BlockSpec((nb, tm, w), lambda b, i: (b, i, 0))


def _block_rows(ref, nb, t0=0, ts=None):
    ts = ref.shape[-2] - t0 if ts is None else ts
    if len(ref.shape) == 3:
        seqs = [ref[b, t0:t0 + ts, :] for b in range(nb)]
    else:
        seqs = [jnp.concatenate([ref[b, c, t0:t0 + ts, :] for c in range(ref.shape[1])], axis=-1)
                for b in range(nb)]
    return seqs[0] if nb == 1 else jnp.concatenate(seqs, axis=0)


def _in_proj_kernel(x_ref, g_ref, w_ref, qkv_ref, z_ref, swa_ref, sc_ref, ab_ref, w_scr, *, nb, tm):
    @pl.when((pl.program_id(0) == 0) & (pl.program_id(1) == 0))
    def _regroup():
        qz = GDN_CONV_CH + GDN_V
        n_gate = 2 * GDN_HEADS
        w_scr[:, 0:qz] = w_ref[:, 0:qz]
        w_scr[:, qz:IN_COLS[-1][0]] = w_ref[:, qz + n_gate:D_IN]
        w_scr[:, IN_COLS[-1][0]:D_IN_PAD] = jnp.concatenate(
            [w_ref[:, qz:qz + n_gate], jnp.zeros((D_MODEL, AB_W - n_gate), BF16)], axis=1)

    xn = _rms(_block_rows(x_ref, nb), g_ref[...]).astype(BF16)
    for o_ref, (lo, hi) in zip((qkv_ref, z_ref, swa_ref, sc_ref, ab_ref), IN_COLS):
        res = jnp.dot(xn, w_scr[:, lo:hi], preferred_element_type=F32)
        for b in range(nb):
            rows = res[b * tm:(b + 1) * tm]
            if len(o_ref.shape) == 4:
                for c in range(o_ref.shape[1]):
                    o_ref[b, c] = rows[:, c * LANES:(c + 1) * LANES]
            else:
                o_ref[b] = rows


def _in_proj(xs, l, p):
    B, T, _ = xs.shape
    nb, tm = _seq_tiles(B, T)
    widths = [hi - lo for lo, hi in IN_COLS]
    slabbed = (True, True, False, False, False)
    out_specs, out_shape = [], []
    for wd, sl in zip(widths, slabbed):
        if sl:
            out_specs.append(pl.BlockSpec((nb, wd // LANES, tm, LANES), lambda b, i: (b, 0, i, 0)))
            out_shape.append(jax.ShapeDtypeStruct((B, wd // LANES, T, LANES), F32))
        else:
            out_specs.append(pl.BlockSpec((nb, tm, wd), lambda b, i: (b, i, 0)))
            out_shape.append(jax.ShapeDtypeStruct((B, T, wd), F32))
    return pl.pallas_call(
        functools.partial(_in_proj_kernel, nb=nb, tm=tm),
        grid=(B // nb, T // tm),
        in_specs=[_tok_spec(nb, tm, D_MODEL), _lspec(l, (1, D_MODEL)),
                  _lspec(l, (D_MODEL, D_IN), single_buffer=True)],
        out_specs=out_specs,
        out_shape=out_shape,
        scratch_shapes=[pltpu.VMEM((D_MODEL, D_IN_PAD), BF16)],
        compiler_params=_params(("arbitrary", "arbitrary")),
        name="in_proj",
    )(xs, p['norm_mix_g'], p['w_in'])


def _cumsum_time(groups):
    prefix = [groups[0]]
    for g in groups[1:]:
        prefix.append(prefix[-1] + g)
    sub = lax.broadcasted_iota(jnp.int32, groups[0].shape, 0)
    run = prefix[-1]
    s = 1
    while s < SUBLANES:
        run = run + jnp.where(sub >= s, pltpu.roll(run, s, axis=0), 0.0)
        s *= 2
    before = jnp.where(sub >= 1, pltpu.roll(run, 1, axis=0), 0.0)
    return [p + before for p in prefix]


def _gdn_kernel(*refs, bb, C, nc, has_state):
    qkv_ref, ab_ref, z_ref, sc_ref, wconv_ref, wsc_ref, hp_ref, gn_ref = refs[:8]
    state_refs = refs[8:11] if has_state else ()
    oa_ref, oc_ref, sout_ref, scout_ref, s_scr, xp_scr, scp_scr = refs[8 + len(state_refs):]
    n = pl.program_id(1)
    pad = SUBLANES
    L = C // SUBLANES
    n_slabs = GDN_CONV_CH // LANES

    def time_of(r):
        return (r % SUBLANES) * L + r // SUBLANES

    def token_row(t):
        return (t % L) * SUBLANES + t // L

    @pl.when(n == 0)
    def _init():
        xp_scr[...] = jnp.zeros_like(xp_scr)
        scp_scr[...] = jnp.zeros_like(scp_scr)
        if has_state:
            s0_ref, gbuf_ref, scbuf_ref = state_refs
            s_scr[...] = s0_ref[...]
            for r in range(GDN_CONV - 1):
                row = token_row(C - (GDN_CONV - 1) + r)
                for b in range(bb):
                    for m in range(n_slabs):
                        xp_scr[b, m, row:row + 1, :] = gbuf_ref[b, r:r + 1, m * LANES:(m + 1) * LANES]
            scp_scr[:, pad - (SC_CONV - 1):pad, :] = scbuf_ref[...]
        else:
            s_scr[...] = jnp.zeros_like(s_scr)

    t_row = time_of(lax.broadcasted_iota(jnp.int32, (C, C), 0))
    t_col = time_of(lax.broadcasted_iota(jnp.int32, (C, C), 1))
    incl = t_row >= t_col
    strict = t_row > t_col
    sub = lax.broadcasted_iota(jnp.int32, (SUBLANES, LANES), 0)
    neg_a = -jnp.exp(hp_ref[0:1, :])
    dt_b = hp_ref[1:2, :]
    n_sq = C.bit_length() - 2

    R = nc * C

    def chunk_groups(window):
        return [window(j) for j in range(L)]

    def shifted(cur, prev, k, cache):
        out = []
        for j in range(L):
            d = j - k
            if d >= 0:
                out.append(cur[d])
                continue
            m = (-d + L - 1) // L
            src = d + m * L
            if (src, m) not in cache:
                cache[src, m] = pltpu.roll(
                    jnp.where(sub >= SUBLANES - m, prev[src], cur[src]), m, axis=0)
            out.append(cache[src, m])
        return jnp.concatenate(out, axis=0)

    ys, gcs, gcts, betas = {}, {}, {}, {}
    for b in range(bb):
        for m in range(n_slabs):
            lanes = slice(m * LANES, (m + 1) * LANES)
            prev = [xp_scr[b, m, j * SUBLANES:(j + 1) * SUBLANES, :] for j in range(L)]
            for c in range(nc):
                cur = chunk_groups(
                    lambda j: qkv_ref[b, m, pl.ds(c * C + j, SUBLANES, stride=L), :])
                cache = {}
                y = shifted(cur, prev, 3, cache) * wconv_ref[0:1, lanes]
                y = y + shifted(cur, prev, 2, cache) * wconv_ref[1:2, lanes]
                y = y + shifted(cur, prev, 1, cache) * wconv_ref[2:3, lanes]
                y = y + jnp.concatenate(cur, axis=0) * wconv_ref[3:4, lanes]
                ys[b, c, m] = _silu(y)
                prev = cur
            xp_scr[b, m] = jnp.concatenate(prev, axis=0)

        for c in range(nc):
            ab = jnp.concatenate(chunk_groups(
                lambda j: ab_ref[b, pl.ds(c * C + j, SUBLANES, stride=L), :]), axis=0)
            t = ab + dt_b
            g = neg_a * (jnp.maximum(t, 0.0) + jnp.log1p(jnp.exp(-jnp.abs(t))))
            betas[b, c] = jax.nn.sigmoid(ab)
            gc = jnp.concatenate(_cumsum_time(
                [g[j * SUBLANES:(j + 1) * SUBLANES] for j in range(L)]), axis=0)
            gcs[b, c] = gc
            gcts[b, c] = jnp.transpose(jnp.concatenate(
                [gc, jnp.zeros((LANES - C, LANES), F32)], axis=0))

        scv = sc_ref[b]
        prod = scv[:, SC_WIDTH:2 * SC_WIDTH] * scv[:, 2 * SC_WIDTH:3 * SC_WIDTH]
        pext = jnp.concatenate([scp_scr[b], prod], axis=0)
        cu = pltpu.roll(pext, 2, axis=0)[pad:] * wsc_ref[0:1, :]
        cu = cu + pltpu.roll(pext, 1, axis=0)[pad:] * wsc_ref[1:2, :]
        cu = cu + prod * wsc_ref[2:3, :]
        scp_scr[b] = prod[R - pad:R, :]
        oc_ref[b] = scv[:, 0:SC_WIDTH] * cu

    problems = [(c, b, h) for c in range(nc) for b in range(bb) for h in range(GDN_HEADS)]
    every = range(len(problems))
    qs, ks, vs, kbs, bts, egs, e_incls, e_stricts, kds, gls = ([] for _ in range(10))
    for c, b, h in problems:
        gc = gcs[b, c]
        q = ys[b, c, h]
        k = ys[b, c, GDN_HEADS + h]
        q = q * lax.rsqrt(jnp.sum(q * q, axis=-1, keepdims=True) + EPS) * (GDN_DK ** -0.5)
        k = k * lax.rsqrt(jnp.sum(k * k, axis=-1, keepdims=True) + EPS)
        gci = gc[:, h:h + 1]
        gcj = gcts[b, c][h:h + 1, 0:C]
        g_last = gc[C - 1:C, h:h + 1]
        e_incl = jnp.exp(jnp.where(incl, gci - gcj, NEG_INF))
        qs.append(q)
        ks.append(k)
        kbs.append(k.astype(BF16))
        vs.append(ys[b, c, 2 * GDN_HEADS + h])
        bts.append(betas[b, c][:, GDN_HEADS + h:GDN_HEADS + h + 1])
        egs.append(jnp.exp(gci))
        e_incls.append(e_incl)
        e_stricts.append(jnp.where(strict, e_incl, 0.0))
        kds.append(k * jnp.exp(g_last - gci))
        gls.append(jnp.exp(g_last))

    kks = [_dot_nt(kbs[i], kbs[i]) for i in every]
    qks = [_dot_nt(qs[i], kbs[i]) * e_incls[i] for i in every]
    p_pows = [-(bts[i] * kks[i] * e_stricts[i]) for i in every]
    n_accs = list(p_pows)
    for _ in range(n_sq):
        pbs = [p.astype(BF16) for p in p_pows]
        p_pows = [_dot(pb, pb) for pb in pbs]
        n_accs = [n_accs[i] + p_pows[i] + _dot(p_pows[i], n_accs[i]) for i in every]
    rhss = [jnp.concatenate([bts[i] * vs[i], (bts[i] * egs[i]) * ks[i]], axis=-1) for i in every]
    sols = [rhss[i] + _dot(n_accs[i], rhss[i]) for i in every]

    pairs = [(b, h) for b in range(bb) for h in range(GDN_HEADS)]
    here = range(len(pairs))
    states = [s_scr[b, h] for b, h in pairs]
    for c in range(nc):
        base = c * len(pairs)
        sbs = [s.astype(BF16) for s in states]
        o_inter = [_dot(qs[base + j] * egs[base + j], sbs[j]) for j in here]
        us = [sols[base + j][:, :GDN_DV] - _dot(sols[base + j][:, GDN_DV:], sbs[j]) for j in here]
        outs = [o_inter[j] + _dot(qks[base + j], us[j]) for j in here]
        states = [gls[base + j] * states[j] + _dot_tn(kds[base + j], us[j]) for j in here]
        for j, (b, h) in enumerate(pairs):
            zh = jnp.concatenate(chunk_groups(
                lambda i: z_ref[b, h, pl.ds(c * C + i, SUBLANES, stride=L), :]), axis=0)
            res = _rms(outs[j], gn_ref[...]) * _silu(zh)
            for i in range(L):
                oa_ref[b, h, pl.ds(c * C + i, SUBLANES, stride=L), :] = res[
                    i * SUBLANES:(i + 1) * SUBLANES]
    for j, (b, h) in enumerate(pairs):
        s_scr[b, h] = states[j]

    @pl.when(n == pl.num_programs(1) - 1)
    def _fin():
        sout_ref[...] = s_scr[...]
        scout_ref[...] = scp_scr[:, pad - (SC_CONV - 1):pad, :]


def _gdn(qkv, ab, z, sc, l, p, states):
    B, n_slabs, T, _ = qkv.shape
    C = CHUNK if T % CHUNK == 0 else T
    assert C % SUBLANES == 0
    bb = 2 if B % 2 == 0 else 1
    nc = GDN_CHUNKS_PER_STEP if (T // C) % GDN_CHUNKS_PER_STEP == 0 else 1
    tok = lambda w: pl.BlockSpec((bb, nc * C, w), lambda i, n: (i, n, 0))
    slabs = lambda k: pl.BlockSpec((bb, k, nc * C, LANES), lambda i, n: (i, 0, n, 0))
    per_b = lambda *s: pl.BlockSpec((bb,) + s, lambda i, n: (i,) + (0,) * len(s))
    per_lb = lambda *s: pl.BlockSpec((None, bb) + s, lambda i, n: (l, i) + (0,) * len(s))
    in_specs = [slabs(n_slabs), tok(AB_W), slabs(GDN_HEADS), tok(SC_W),
                _lspec(l, (GDN_CONV, GDN_CONV_CH)), _lspec(l, (SC_CONV, SC_WIDTH)),
                _lspec(l, (SUBLANES, LANES)), _lspec(l, (1, GDN_DV))]
    args = [qkv, ab, z, sc, p['w_gdn_conv'], p['w_sc_conv'], p['hp'], p['gdn_norm_g']]
    if states is not None:
        in_specs += [per_lb(GDN_HEADS, GDN_DK, GDN_DV), per_lb(GDN_CONV - 1, GDN_CONV_CH),
                     per_lb(SC_CONV - 1, SC_WIDTH)]
        args += list(states)
    return pl.pallas_call(
        functools.partial(_gdn_kernel, bb=bb, C=C, nc=nc, has_state=states is not None),
        grid=(B // bb, T // (nc * C)),
        in_specs=in_specs,
        out_specs=[slabs(GDN_HEADS), tok(SC_WIDTH),
                   per_b(GDN_HEADS, GDN_DK, GDN_DV), per_b(SC_CONV - 1, SC_WIDTH)],
        out_shape=[jax.ShapeDtypeStruct((B, GDN_HEADS, T, GDN_DV), F32),
                   jax.ShapeDtypeStruct((B, T, SC_WIDTH), F32),
                   jax.ShapeDtypeStruct((B, GDN_HEADS, GDN_DK, GDN_DV), F32),
                   jax.ShapeDtypeStruct((B, SC_CONV - 1, SC_WIDTH), F32)],
        scratch_shapes=[pltpu.VMEM((bb, GDN_HEADS, GDN_DK, GDN_DV), F32),
                        pltpu.VMEM((bb, n_slabs, C, LANES), F32),
                        pltpu.VMEM((bb, SUBLANES, SC_WIDTH), F32)],
        compiler_params=_params(("arbitrary", "arbitrary")),
        name="gdn",
    )(*args)


def _swa_norm_k(x, gk):
    return jnp.concatenate(
        [_rms(x[:, SWA_Q + i * SWA_HD:SWA_Q + (i + 1) * SWA_HD], gk) for i in range(SWA_KV_HEADS)],
        axis=-1)


def _swa_norm_q(x, gq):
    return [_rms(x[:, h * SWA_HD:(h + 1) * SWA_HD], gq) for h in range(SWA_HEADS)]


def _group_queries(qn, g, r0, r1, sinks):
    heads = range(g * SWA_GROUP, (g + 1) * SWA_GROUP)
    q = jnp.concatenate([qn[h][r0:r1] for h in heads], axis=0)
    sink = jnp.concatenate(
        [jnp.broadcast_to(sinks[:, h:h + 1], (r1 - r0, 1)) for h in heads], axis=0)
    return q, sink


def _attend_many(problems):
    scores = [_dot_nt(q, k) * (SWA_HD ** -0.5) for q, k, _, _, _ in problems]
    probs = []
    for s, (_, _, _, valid, sink) in zip(scores, problems):
        if valid is not None:
            s = jnp.where(valid, s, NEG_INF)
        m = jnp.maximum(jnp.max(s, axis=-1, keepdims=True), sink)
        p = jnp.exp(s - m)
        probs.append(p / (jnp.sum(p, axis=-1, keepdims=True) + jnp.exp(sink - m)))
    return [_dot(p, pr[2]) for p, pr in zip(probs, problems)]


def _ungroup(outs, rows):
    return jnp.concatenate(
        [o[j * rows:(j + 1) * rows] for o in outs for j in range(SWA_GROUP)], axis=-1)


def _split_bf16(x):
    hi = x.astype(BF16)
    return hi, (x - hi.astype(F32)).astype(BF16)


def _sum_dot(x, ones):
    hi, lo = _split_bf16(x)
    return (jnp.dot(hi, ones, preferred_element_type=F32)
            + jnp.dot(lo, ones, preferred_element_type=F32))


def _head_rms(x, g_tiled):
    w = x.shape[1]
    same_head = (lax.broadcasted_iota(jnp.int32, (w, w), 0) // SWA_HD
                 == lax.broadcasted_iota(jnp.int32, (w, w), 1) // SWA_HD)
    ss = _sum_dot(x * x, jnp.where(same_head, 1.0, 0.0).astype(BF16))
    return x * lax.rsqrt(ss * (1.0 / SWA_HD) + EPS) * g_tiled


def _swa_prompt_kernel(x_ref, halo_ref, gq_ref, gk_ref, sink_ref, o_ref, kn_ref, *, qb):
    i = pl.program_id(1)
    band = (WIN_CHUNKS + 1) * CHUNK
    n_chunks = qb // CHUNK
    x = x_ref[...]
    hx = halo_ref[...]
    kcols = slice(SWA_Q, SWA_Q + SWA_KV)
    vcols = slice(SWA_Q + SWA_KV, SWA_W)
    qn = _head_rms(x[:, 0:SWA_Q], gq_ref[...])
    k_all = _head_rms(jnp.concatenate([hx[:, kcols], x[:, kcols]], axis=0), gk_ref[...])
    v_all = jnp.concatenate([hx[:, vcols], x[:, vcols]], axis=0)
    kn_ref[...] = k_all[WINDOW:]

    lane_half = lax.broadcasted_iota(jnp.int32, (1, SWA_KV), 1) // SWA_HD

    def variants(a):
        rot = pltpu.roll(a, SWA_HD, axis=1)
        return [[jnp.where(lane_half == j, a if j == g else rot, 0.0).astype(BF16)
                 for j in range(SWA_GROUP)] for g in range(SWA_KV_HEADS)]

    k_var, v_var = variants(k_all), variants(v_all)
    v_ones = [[jnp.concatenate([v, jnp.ones((WINDOW + qb, SWA_KV), BF16)], axis=1) for v in vg]
              for vg in v_var]
    local_chunk = lax.broadcasted_iota(jnp.int32, (1, band), 1) // CHUNK
    problems = [(c, g, j) for c in range(n_chunks) for g in range(SWA_KV_HEADS)
                for j in range(SWA_GROUP)]
    scores = []
    for c, g, j in problems:
        q = qn[c * CHUNK:(c + 1) * CHUNK, g * SWA_KV:(g + 1) * SWA_KV]
        scores.append(_dot_nt(q, k_var[g][j][c * CHUNK:c * CHUNK + band]) * (SWA_HD ** -0.5))
    ps, tails = [], []
    for s, (c, g, j) in zip(scores, problems):
        valid = local_chunk + (i * n_chunks + c - WIN_CHUNKS) >= 0
        s = jnp.where(valid, s, NEG_INF)
        sink = sink_ref[:, g * SWA_GROUP + j:g * SWA_GROUP + j + 1]
        m = jnp.maximum(jnp.max(s, axis=-1, keepdims=True), sink)
        ps.append(jnp.exp(s - m))
        tails.append(jnp.exp(sink - m))
    outs = []
    for p, tail, (c, g, j) in zip(ps, tails, problems):
        p_hi, p_lo = _split_bf16(p)
        pv_sum = jnp.dot(jnp.concatenate([p_hi, p_lo], axis=0),
                         v_ones[g][j][c * CHUNK:c * CHUNK + band], preferred_element_type=F32)
        denom = (pv_sum[:CHUNK, SWA_KV:] + pv_sum[CHUNK:, SWA_KV:]) + tail
        outs.append(pv_sum[:CHUNK, :SWA_KV] / denom)
    for c in range(n_chunks):
        for g in range(SWA_KV_HEADS):
            k0 = (c * SWA_KV_HEADS + g) * SWA_GROUP
            o_ref[c * CHUNK:(c + 1) * CHUNK, g * SWA_KV:(g + 1) * SWA_KV] = sum(
                outs[k0 + 1:k0 + SWA_GROUP], outs[k0])


def _swa_prompt(swa, l, p):
    B, T, _ = swa.shape
    qb = min(SWA_QUERY_BLOCK, T)
    assert T % qb == 0 and qb % WINDOW == 0
    halo_per_block = qb // WINDOW
    return pl.pallas_call(
        functools.partial(_swa_prompt_kernel, qb=qb),
        grid=(B, T // qb),
        in_specs=[pl.BlockSpec((None, qb, SWA_W), lambda b, i: (b, i, 0)),
                  pl.BlockSpec((None, WINDOW, SWA_W),
                               lambda b, i: (b, jnp.maximum(i * halo_per_block - 1, 0), 0)),
                  _lspec(l, (1, SWA_Q)), _lspec(l, (1, SWA_KV)), _lspec(l, (1, LANES))],
        out_specs=[pl.BlockSpec((None, qb, SWA_Q), lambda b, i: (b, i, 0)),
                   pl.BlockSpec((None, qb, SWA_KV), lambda b, i: (b, i, 0))],
        out_shape=[jax.ShapeDtypeStruct((B, T, SWA_Q), F32),
                   jax.ShapeDtypeStruct((B, T, SWA_KV), F32)],
        compiler_params=_params(("arbitrary", "arbitrary")),
        name="swa_prompt",
    )(swa, swa, p['swa_q_norm_g_tiled'], p['swa_k_norm_g_tiled'], p['swa_sinks'])


def _swa_sample_kernel(x_ref, kp_ref, vp_ref, gq_ref, gk_ref, sink_ref, o_ref, kn_ref, *, bb, T):
    problems = []
    for b in range(bb):
        x = x_ref[b]
        kn = _swa_norm_k(x, gk_ref[...])
        kn_ref[b] = kn
        k_all = jnp.concatenate([kp_ref[b], kn], axis=0)
        v_all = jnp.concatenate([vp_ref[b], x[:, SWA_Q + SWA_KV:SWA_W]], axis=0)
        qn = _swa_norm_q(x, gq_ref[...])
        for g in range(SWA_KV_HEADS):
            q, sink = _group_queries(qn, g, 0, T, sink_ref[...])
            cols = slice(g * SWA_HD, (g + 1) * SWA_HD)
            problems.append((q, k_all[:, cols], v_all[:, cols], None, sink))
    outs = _attend_many(problems)
    for b in range(bb):
        o_ref[b] = _ungroup(outs[b * SWA_KV_HEADS:(b + 1) * SWA_KV_HEADS], T)


def _swa_sample(swa, k_past, v_past, l, p):
    B, T, _ = swa.shape
    past = k_past.shape[2]
    bb = 4 if B % 4 == 0 else 1
    blk = lambda r, w: pl.BlockSpec((bb, r, w), lambda b: (b, 0, 0))
    cache = pl.BlockSpec((None, bb, past, SWA_KV), lambda b: (l, b, 0, 0))
    return pl.pallas_call(
        functools.partial(_swa_sample_kernel, bb=bb, T=T),
        grid=(B // bb,),
        in_specs=[blk(T, SWA_W), cache, cache,
                  _lspec(l, (1, SWA_HD)), _lspec(l, (1, SWA_HD)), _lspec(l, (1, LANES))],
        out_specs=[blk(T, SWA_Q), blk(T, SWA_KV)],
        out_shape=[jax.ShapeDtypeStruct((B, T, SWA_Q), F32),
                   jax.ShapeDtypeStruct((B, T, SWA_KV), F32)],
        compiler_params=_params(("arbitrary",)),
        name="swa_sample",
    )(swa, k_past, v_past, p['swa_q_norm_g'], p['swa_k_norm_g'], p['swa_sinks'])


def _mem_kv_kernel(m_ref, g_ref, wk_ref, wv_ref, gk_ref, k_ref, v_ref):
    m = _rms(m_ref[...], g_ref[...]).astype(BF16)
    k = jnp.dot(m, wk_ref[...], preferred_element_type=F32)
    k_ref[...] = jnp.concatenate(
        [_rms(k[:, h * MEM_HD:(h + 1) * MEM_HD], gk_ref[...]) for h in range(MEM_HEADS)], axis=-1)
    v_ref[...] = jnp.dot(m, wv_ref[...], preferred_element_type=F32)


def _mem_kv(mem, l, p):
    B, M, _ = mem.shape
    blk = pl.BlockSpec((None, M, D_MODEL), lambda b: (b, 0, 0))
    return pl.pallas_call(
        _mem_kv_kernel,
        grid=(B,),
        in_specs=[blk, _lspec(l, (1, D_MODEL)), _lspec(l, (D_MODEL, D_MODEL)),
                  _lspec(l, (D_MODEL, D_MODEL)), _lspec(l, (1, MEM_HD))],
        out_specs=[blk, blk],
        out_shape=[jax.ShapeDtypeStruct((B, M, D_MODEL), F32)] * 2,
        compiler_params=_params(("arbitrary",)),
        name="mem_kv",
    )(mem, p['mem_in_norm_g'], p['w_mk'], p['w_mv'], p['mem_k_norm_g'])


def _post_kernel(x_ref, a_ref, b_ref, c_ref, wo_ref, gm_ref, wq_ref, gq_ref, mk_ref, mv_ref,
                 wmo_ref, o_ref, *, nb, tm, n_sub):
    ts = tm // n_sub
    probs = [(b, slice(hd * MEM_HD, (hd + 1) * MEM_HD)) for b in range(nb) for hd in range(MEM_HEADS)]
    hs, scores = [], []
    for sub in range(n_sub):
        rows = lambda ref: _block_rows(ref, nb, sub * ts, ts)
        h = rows(x_ref)
        h = h + (jnp.dot(rows(a_ref).astype(BF16), wo_ref[0:GDN_V, :], preferred_element_type=F32)
                 + jnp.dot(rows(b_ref).astype(BF16), wo_ref[GDN_V:GDN_V + SWA_Q, :],
                           preferred_element_type=F32)
                 + jnp.dot(rows(c_ref).astype(BF16), wo_ref[GDN_V + SWA_Q:, :],
                           preferred_element_type=F32))
        hs.append(h)
        hn = _rms(h, gm_ref[...]).astype(BF16)
        q = jnp.dot(hn, wq_ref[...], preferred_element_type=F32)
        qhs = [_rms(q[b * ts:(b + 1) * ts, sl], gq_ref[...]) for b, sl in probs]
        scores.append([_dot_nt(qh, mk_ref[b, :, sl]) * (MEM_HD ** -0.5)
                       for qh, (b, sl) in zip(qhs, probs)])
    ps = []
    for sub in range(n_sub):
        ps.append([])
        for s in scores[sub]:
            p = jnp.exp(s - jnp.max(s, axis=-1, keepdims=True))
            ps[sub].append(p / jnp.sum(p, axis=-1, keepdims=True))
    for sub in range(n_sub):
        outs = [_dot(p, mv_ref[b, :, sl]) for p, (b, sl) in zip(ps[sub], probs)]
        o = jnp.concatenate(
            [jnp.concatenate(outs[b * MEM_HEADS:(b + 1) * MEM_HEADS], axis=-1) for b in range(nb)],
            axis=0).astype(BF16)
        res = hs[sub] + jnp.dot(o, wmo_ref[...], preferred_element_type=F32)
        for b in range(nb):
            o_ref[b, sub * ts:(sub + 1) * ts, :] = res[b * ts:(b + 1) * ts]


def _post(xs, oa, ob, oc, mk, mv, lm, l, p):
    B, T, _ = xs.shape
    nb, tm, n_sub = _sub_tiles(B, T)
    tok = lambda w: _tok_spec(nb, tm, w)
    mem = pl.BlockSpec((None, nb, MEM_TOKENS, D_MODEL), lambda b, i: (lm, b, 0, 0))
    sq = _lspec(l, (D_MODEL, D_MODEL), single_buffer=True)
    return pl.pallas_call(
        functools.partial(_post_kernel, nb=nb, tm=tm, n_sub=n_sub),
        grid=(B // nb, T // tm),
        in_specs=[tok(D_MODEL),
                  pl.BlockSpec((nb, GDN_HEADS, tm, GDN_DV), lambda b, i: (b, 0, i, 0)),
                  tok(SWA_Q), tok(SC_WIDTH), sq,
                  _lspec(l, (1, D_MODEL)), sq, _lspec(l, (1, MEM_HD)), mem, mem, sq],
        out_specs=tok(D_MODEL),
        out_shape=jax.ShapeDtypeStruct((B, T, D_MODEL), F32),
        compiler_params=_params(("arbitrary", "arbitrary")),
        name="post",
    )(xs, oa, ob, oc, p['w_o'], p['norm_mem_g'], p['w_mq'], p['mem_q_norm_g'], mk, mv, p['w_mo'])


def _ffn_kernel(*refs, nb, tm, n_sub, has_state):
    h_ref, g_ref, wup_ref, wc_ref, wdn_ref = refs[:5]
    buf_ref = refs[5] if has_state else None
    o_ref, st_ref, carry_scr = refs[5 + int(has_state):]
    i = pl.program_id(1)
    ts = tm // n_sub

    @pl.when(i == 0)
    def _init():
        carry_scr[...] = jnp.zeros_like(carry_scr)
        if has_state:
            for b in range(nb):
                u_m2, u_m1 = buf_ref[b, 0:1, :], buf_ref[b, 1:2, :]
                carry_scr[b, 0:1, :] = u_m1 * wc_ref[0:1, :]
                carry_scr[b, 1:2, :] = u_m1 * wc_ref[1:2, :] + u_m2 * wc_ref[0:1, :]
                carry_scr[b, 2:4, :] = buf_ref[b]

    def delayed(x, carry_row, cols):
        rolled = pltpu.roll(x, 1, axis=0)
        first_row = lax.broadcasted_iota(jnp.int32, (SUBLANES, x.shape[1]), 0) == 0
        pieces = []
        for b in range(nb):
            r0 = b * ts
            pieces.append(jnp.where(first_row, carry_scr[b, carry_row:carry_row + 1, cols],
                                    rolled[r0:r0 + SUBLANES]))
            pieces.append(rolled[r0 + SUBLANES:r0 + ts])
        return jnp.concatenate(pieces, axis=0)

    def conv_act(xn, lo, hi):
        conv = []
        for base in (0, D_FF):
            cols = slice(base + lo, base + hi)
            u = jnp.dot(xn, wup_ref[:, cols], preferred_element_type=F32)
            first = u * wc_ref[0:1, cols]
            mid = u * wc_ref[1:2, cols] + delayed(first, 0, cols)
            y = delayed(mid, 1, cols) + u * wc_ref[2:3, cols]
            for b in range(nb):
                last = (b + 1) * ts - 1
                carry_scr[b, 0:1, cols] = first[last:last + 1]
                carry_scr[b, 1:2, cols] = mid[last:last + 1]
                carry_scr[b, 2:4, cols] = u[last - 1:last + 1]
            conv.append(y)
        return (_silu(conv[0]) * conv[1]).astype(BF16)

    def down(acc, act, lo, hi):
        return acc + jnp.dot(act, wdn_ref[lo:hi, :], preferred_element_type=F32)

    n_ch = len(FFN_COL_CHUNKS)
    for sub in range(n_sub):
        t0 = sub * ts
        hs = _block_rows(h_ref, nb, t0, ts)
        xn = _rms(hs, g_ref[...]).astype(BF16)
        acc = hs
        act = conv_act(xn, *FFN_COL_CHUNKS[0])
        for k in range(n_ch):
            nxt = conv_act(xn, *FFN_COL_CHUNKS[k + 1]) if k + 1 < n_ch else None
            acc = down(acc, act, *FFN_COL_CHUNKS[k])
            act = nxt
        for b in range(nb):
            o_ref[b, t0:t0 + ts, :] = acc[b * ts:(b + 1) * ts]

    @pl.when(i == pl.num_programs(1) - 1)
    def _fin():
        for b in range(nb):
            st_ref[b] = carry_scr[b, 2:4, :]


def _ffn(h, l, p, state):
    B, T, _ = h.shape
    nb, tm, n_sub = _sub_tiles(B, T)
    assert (tm // n_sub) % (2 * SUBLANES) == 0
    tok = _tok_spec(nb, tm, D_MODEL)
    in_specs = [tok, _lspec(l, (1, D_MODEL)), _lspec(l, (D_MODEL, 2 * D_FF), single_buffer=True),
                _lspec(l, (FFN_CONV, 2 * D_FF)), _lspec(l, (D_FF, D_MODEL), single_buffer=True)]
    args = [h, p['norm_ffn_g'], p['w_up'], p['w_ffn_conv'], p['w_down']]
    if state is not None:
        in_specs.append(pl.BlockSpec((None, nb, FFN_CONV - 1, 2 * D_FF), lambda b, i: (l, b, 0, 0)))
        args.append(state)
    return pl.pallas_call(
        functools.partial(_ffn_kernel, nb=nb, tm=tm, n_sub=n_sub, has_state=state is not None),
        grid=(B // nb, T // tm),
        in_specs=in_specs,
        out_specs=[tok, pl.BlockSpec((nb, FFN_CONV - 1, 2 * D_FF), lambda b, i: (b, 0, 0))],
        out_shape=[jax.ShapeDtypeStruct(h.shape, F32),
                   jax.ShapeDtypeStruct((B, FFN_CONV - 1, 2 * D_FF), F32)],
        scratch_shapes=[pltpu.VMEM((nb, SUBLANES, 2 * D_FF), F32)],
        compiler_params=_params(("arbitrary", "arbitrary")),
        name="ffn",
    )(*args)


def _layer(x, l, p, mk, mv, lm, gdn_states, ffn_state, swa_cache, keep):
    B, T, _ = x.shape
    qkv, z, swa, sc, ab = _in_proj(x, l, p)
    oa, oc, s_new, sc_new = _gdn(qkv, ab, z, sc, l, p, gdn_states)
    if swa_cache is None:
        ob, kn = _swa_prompt(swa, l, p)
    else:
        ob, kn = _swa_sample(swa, swa_cache[0], swa_cache[1], l, p)
    h = _post(x, oa, ob, oc, mk, mv, lm, l, p)
    h, f_new = _ffn(h, l, p, ffn_state)
    gb_new = qkv[:, :, T - (GDN_CONV - 1):, :].transpose(0, 2, 1, 3).reshape(
        B, GDN_CONV - 1, GDN_CONV_CH)
    sk = kn[:, T - keep:].reshape(B, keep, SWA_KV_HEADS, SWA_HD)
    sv = swa[:, T - keep:, SWA_Q + SWA_KV:].reshape(B, keep, SWA_KV_HEADS, SWA_HD)
    return h, s_new, gb_new, sc_new, f_new, sk, sv


def kernel(x_prompt, x_sample, mem_prompt, state_gdn, state_gdn_conv, cache_swa_k, cache_swa_v, state_sc_conv, cache_mem_k, cache_mem_v, state_ffn_conv, norm_mix_g, w_in, w_gdn_conv, gdn_a_log, gdn_dt_bias, gdn_norm_g, swa_q_norm_g, swa_k_norm_g, swa_sinks, w_sc_conv, w_o, norm_mem_g, mem_in_norm_g, w_mq, w_mk, w_mv, mem_q_norm_g, mem_k_norm_g, w_mo, norm_ffn_g, w_up, w_ffn_conv, w_down):
    depth = w_in.shape[0]
    Bp, Tp, _ = x_prompt.shape
    Bs, Ts, _ = x_sample.shape
    keep = min(WINDOW, Tp)
    past = cache_swa_k.shape[2]

    hp = jnp.zeros((depth, SUBLANES, LANES), F32)
    hp = hp.at[:, 0, :GDN_HEADS].set(gdn_a_log).at[:, 1, :GDN_HEADS].set(gdn_dt_bias)
    sinks = jnp.zeros((depth, 1, LANES), F32).at[:, 0, :SWA_HEADS].set(swa_sinks)
    row = lambda a: a[:, None, :]
    p = {
        'norm_mix_g': row(norm_mix_g), 'w_in': w_in.astype(BF16), 'w_gdn_conv': w_gdn_conv, 'hp': hp,
        'gdn_norm_g': row(gdn_norm_g), 'swa_q_norm_g': row(swa_q_norm_g),
        'swa_k_norm_g': row(swa_k_norm_g), 'swa_sinks': sinks,
        'swa_q_norm_g_tiled': row(jnp.tile(swa_q_norm_g, (1, SWA_HEADS))),
        'swa_k_norm_g_tiled': row(jnp.tile(swa_k_norm_g, (1, SWA_KV_HEADS))), 'w_sc_conv': w_sc_conv,
        'w_o': w_o.astype(BF16), 'norm_mem_g': row(norm_mem_g), 'mem_in_norm_g': row(mem_in_norm_g),
        'w_mq': w_mq.astype(BF16), 'w_mk': w_mk.astype(BF16), 'w_mv': w_mv.astype(BF16),
        'mem_q_norm_g': row(mem_q_norm_g), 'mem_k_norm_g': row(mem_k_norm_g),
        'w_mo': w_mo.astype(BF16), 'norm_ffn_g': row(norm_ffn_g), 'w_up': w_up.astype(BF16),
        'w_ffn_conv': w_ffn_conv, 'w_down': w_down.astype(BF16),
    }
    cache_k = cache_swa_k.reshape(depth, Bs, past, SWA_KV)
    cache_v = cache_swa_v.reshape(depth, Bs, past, SWA_KV)
    mem_k_s = cache_mem_k.reshape(depth, Bs, MEM_TOKENS, D_MODEL)
    mem_v_s = cache_mem_v.reshape(depth, Bs, MEM_TOKENS, D_MODEL)

    hp_, hs_ = x_prompt, x_sample
    outs_p = {k: [] for k in ('S', 'gb', 'sk', 'sv', 'sb', 'mk', 'mv', 'fb')}
    outs_s = {k: [] for k in ('S', 'gb', 'sk', 'sv', 'sb', 'fb')}
    for l in range(depth):
        mk, mv = _mem_kv(mem_prompt, l, p)
        hp_, S, gb, sb, fb, sk, sv = _layer(hp_, l, p, mk[None], mv[None], 0, None, None, None, keep)
        outs_p['S'].append(S); outs_p['gb'].append(gb); outs_p['sb'].append(sb); outs_p['fb'].append(fb)
        outs_p['sk'].append(sk); outs_p['sv'].append(sv)
        outs_p['mk'].append(mk.reshape(Bp, MEM_TOKENS, MEM_HEADS, MEM_HD))
        outs_p['mv'].append(mv.reshape(Bp, MEM_TOKENS, MEM_HEADS, MEM_HD))
        hs_, S, gb, sb, fb, sk, sv = _layer(
            hs_, l, p, mem_k_s, mem_v_s, l, (state_gdn, state_gdn_conv, state_sc_conv),
            state_ffn_conv, (cache_k, cache_v), Ts)
        outs_s['S'].append(S); outs_s['gb'].append(gb); outs_s['sb'].append(sb); outs_s['fb'].append(fb)
        outs_s['sk'].append(sk); outs_s['sv'].append(sv)
    st = jnp.stack
    return (hp_, hs_,
            st(outs_p['S']), st(outs_p['gb']), st(outs_p['sk']), st(outs_p['sv']), st(outs_p['sb']),
            st(outs_p['mk']), st(outs_p['mv']), st(outs_p['fb']),
            st(outs_s['S']), st(outs_s['gb']), st(outs_s['sk']), st(outs_s['sv']), st(outs_s['sb']),
            st(outs_s['fb']))
```

```python
import functools

import jax
import jax.numpy as jnp
from jax import lax
from jax.experimental import pallas as pl
from jax.experimental.pallas import tpu as pltpu

F32 = jnp.float32
BF16 = jnp.bfloat16

D_MODEL = 1024
CHUNK = 64
EPS = 1e-6
NEG_INF = -1e30
GDN_HEADS = 4
GDN_DK = 128
GDN_DV = 128
GDN_CONV = 4
GDN_QK = GDN_HEADS * GDN_DK
GDN_V = GDN_HEADS * GDN_DV
GDN_CONV_CH = 2 * GDN_QK + GDN_V
SWA_HEADS = 4
SWA_KV_HEADS = 2
SWA_GROUP = SWA_HEADS // SWA_KV_HEADS
SWA_HD = 64
WINDOW = 128
WIN_CHUNKS = WINDOW // CHUNK
SWA_Q = SWA_HEADS * SWA_HD
SWA_KV = SWA_KV_HEADS * SWA_HD
SWA_W = SWA_Q + 2 * SWA_KV
assert SWA_GROUP == SWA_KV_HEADS == 2 and SWA_KV == 128
SC_WIDTH = 256
SC_CONV = 3
MEM_TOKENS = 256
MEM_HEADS = 4
MEM_HD = D_MODEL // MEM_HEADS
D_FF = 2816
FFN_CONV = 3
LANES = 128
SUBLANES = 8
AB_W = LANES
SC_W = 3 * SC_WIDTH
IN_COLS = (
    (0, GDN_CONV_CH),
    (GDN_CONV_CH, GDN_CONV_CH + GDN_V),
    (GDN_CONV_CH + GDN_V, GDN_CONV_CH + GDN_V + SWA_W),
    (GDN_CONV_CH + GDN_V + SWA_W, GDN_CONV_CH + GDN_V + SWA_W + SC_W),
    (GDN_CONV_CH + GDN_V + SWA_W + SC_W, GDN_CONV_CH + GDN_V + SWA_W + SC_W + AB_W),
)
D_IN_PAD = IN_COLS[-1][1]
D_IN = D_IN_PAD - AB_W + 2 * GDN_HEADS
FFN_COL_CHUNKS = ((0, 1024), (1024, 2048), (2048, D_FF))
GDN_CHUNKS_PER_STEP = 4
SUB_TILES = 2
VMEM_LIMIT = 56 * 1024 * 1024
TOKEN_TILE = 512
SWA_QUERY_BLOCK = 2048
MAX_ROWS_SMALL_SEQ = 128


def _seq_tiles(B, T):
    if T >= TOKEN_TILE:
        return 1, TOKEN_TILE
    nb = max(1, min(B, MAX_ROWS_SMALL_SEQ // T))
    while B % nb:
        nb -= 1
    return nb, T


def _sub_tiles(B, T):
    nb, tm = _seq_tiles(B, T)
    if nb == 1 and T % (SUB_TILES * tm) == 0:
        return nb, SUB_TILES * tm, SUB_TILES
    return nb, tm, 1


def _dot(a, b):
    return jnp.dot(a.astype(BF16), b.astype(BF16), preferred_element_type=F32)


def _dot_nt(a, b):
    return lax.dot_general(a.astype(BF16), b.astype(BF16), (((1,), (1,)), ((), ())),
                           preferred_element_type=F32)


def _dot_tn(a, b):
    return lax.dot_general(a.astype(BF16), b.astype(BF16), (((0,), (0,)), ((), ())),
                           preferred_element_type=F32)


def _rms(x, g):
    return x * lax.rsqrt(jnp.mean(x * x, axis=-1, keepdims=True) + EPS) * g


def _silu(x):
    return x * jax.nn.sigmoid(x)


def _lspec(l, shape, single_buffer=False):
    nd = len(shape)
    mode = dict(pipeline_mode=pl.Buffered(1)) if single_buffer else {}
    return pl.BlockSpec((None,) + tuple(shape), lambda *_: (l,) + (0,) * nd, **mode)


def _params(sem):
    return pltpu.CompilerParams(dimension_semantics=sem, vmem_limit_bytes=VMEM_LIMIT)


def _tok_spec(nb, tm, w):
    return pl.BlockSpec((nb, tm, w), lambda b, i: (b, i, 0))


def _block_rows(ref, nb, t0=0, ts=None):
    ts = ref.shape[-2] - t0 if ts is None else ts
    if len(ref.shape) == 3:
        seqs = [ref[b, t0:t0 + ts, :] for b in range(nb)]
    else:
        seqs = [jnp.concatenate([ref[b, c, t0:t0 + ts, :] for c in range(ref.shape[1])], axis=-1)
                for b in range(nb)]
    return seqs[0] if nb == 1 else jnp.concatenate(seqs, axis=0)


def _in_proj_kernel(x_ref, g_ref, w_ref, qkv_ref, z_ref, swa_ref, sc_ref, ab_ref, w_scr, *, nb, tm):
    @pl.when((pl.program_id(0) == 0) & (pl.program_id(1) == 0))
    def _regroup():
        qz = GDN_CONV_CH + GDN_V
        n_gate = 2 * GDN_HEADS
        w_scr[:, 0:qz] = w_ref[:, 0:qz]
        w_scr[:, qz:IN_COLS[-1][0]] = w_ref[:, qz + n_gate:D_IN]
        w_scr[:, IN_COLS[-1][0]:D_IN_PAD] = jnp.concatenate(
            [w_ref[:, qz:qz + n_gate], jnp.zeros((D_MODEL, AB_W - n_gate), BF16)], axis=1)

    xn = _rms(_block_rows(x_ref, nb), g_ref[...]).astype(BF16)
    for o_ref, (lo, hi) in zip((qkv_ref, z_ref, swa_ref, sc_ref, ab_ref), IN_COLS):
        res = jnp.dot(xn, w_scr[:, lo:hi], preferred_element_type=F32)
        for b in range(nb):
            rows = res[b * tm:(b + 1) * tm]
            if len(o_ref.shape) == 4:
                for c in range(o_ref.shape[1]):
                    o_ref[b, c] = rows[:, c * LANES:(c + 1) * LANES]
            else:
                o_ref[b] = rows


def _in_proj(xs, l, p):
    B, T, _ = xs.shape
    nb, tm = _seq_tiles(B, T)
    widths = [hi - lo for lo, hi in IN_COLS]
    slabbed = (True, True, False, False, False)
    out_specs, out_shape = [], []
    for wd, sl in zip(widths, slabbed):
        if sl:
            out_specs.append(pl.BlockSpec((nb, wd // LANES, tm, LANES), lambda b, i: (b, 0, i, 0)))
            out_shape.append(jax.ShapeDtypeStruct((B, wd // LANES, T, LANES), F32))
        else:
            out_specs.append(pl.BlockSpec((nb, tm, wd), lambda b, i: (b, i, 0)))
            out_shape.append(jax.ShapeDtypeStruct((B, T, wd), F32))
    return pl.pallas_call(
        functools.partial(_in_proj_kernel, nb=nb, tm=tm),
        grid=(B // nb, T // tm),
        in_specs=[_tok_spec(nb, tm, D_MODEL), _lspec(l, (1, D_MODEL)),
                  _lspec(l, (D_MODEL, D_IN), single_buffer=True)],
        out_specs=out_specs,
        out_shape=out_shape,
        scratch_shapes=[pltpu.VMEM((D_MODEL, D_IN_PAD), BF16)],
        compiler_params=_params(("arbitrary", "arbitrary")),
        name="in_proj",
    )(xs, p['norm_mix_g'], p['w_in'])


def _cumsum_time(groups):
    prefix = [groups[0]]
    for g in groups[1:]:
        prefix.append(prefix[-1] + g)
    sub = lax.broadcasted_iota(jnp.int32, groups[0].shape, 0)
    run = prefix[-1]
    s = 1
    while s < SUBLANES:
        run = run + jnp.where(sub >= s, pltpu.roll(run, s, axis=0), 0.0)
        s *= 2
    before = jnp.where(sub >= 1, pltpu.roll(run, 1, axis=0), 0.0)
    return [p + before for p in prefix]


def _gdn_kernel(*refs, bb, C, nc, has_state):
    qkv_ref, ab_ref, z_ref, sc_ref, wconv_ref, wsc_ref, hp_ref, gn_ref = refs[:8]
    state_refs = refs[8:11] if has_state else ()
    oa_ref, oc_ref, sout_ref, scout_ref, s_scr, xp_scr, scp_scr = refs[8 + len(state_refs):]
    n = pl.program_id(1)
    pad = SUBLANES
    L = C // SUBLANES
    n_slabs = GDN_CONV_CH // LANES

    def time_of(r):
        return (r % SUBLANES) * L + r // SUBLANES

    def token_row(t):
        return (t % L) * SUBLANES + t // L

    @pl.when(n == 0)
    def _init():
        xp_scr[...] = jnp.zeros_like(xp_scr)
        scp_scr[...] = jnp.zeros_like(scp_scr)
        if has_state:
            s0_ref, gbuf_ref, scbuf_ref = state_refs
            s_scr[...] = s0_ref[...]
            for r in range(GDN_CONV - 1):
                row = token_row(C - (GDN_CONV - 1) + r)
                for b in range(bb):
                    for m in range(n_slabs):
                        xp_scr[b, m, row:row + 1, :] = gbuf_ref[b, r:r + 1, m * LANES:(m + 1) * LANES]
            scp_scr[:, pad - (SC_CONV - 1):pad, :] = scbuf_ref[...]
        else:
            s_scr[...] = jnp.zeros_like(s_scr)

    t_row = time_of(lax.broadcasted_iota(jnp.int32, (C, C), 0))
    t_col = time_of(lax.broadcasted_iota(jnp.int32, (C, C), 1))
    incl = t_row >= t_col
    strict = t_row > t_col
    sub = lax.broadcasted_iota(jnp.int32, (SUBLANES, LANES), 0)
    neg_a = -jnp.exp(hp_ref[0:1, :])
    dt_b = hp_ref[1:2, :]
    n_sq = C.bit_length() - 2

    R = nc * C

    def chunk_groups(window):
        return [window(j) for j in range(L)]

    def shifted(cur, prev, k, cache):
        out = []
        for j in range(L):
            d = j - k
            if d >= 0:
                out.append(cur[d])
                continue
            m = (-d + L - 1) // L
            src = d + m * L
            if (src, m) not in cache:
                cache[src, m] = pltpu.roll(
                    jnp.where(sub >= SUBLANES - m, prev[src], cur[src]), m, axis=0)
            out.append(cache[src, m])
        return jnp.concatenate(out, axis=0)

    ys, gcs, gcts, betas = {}, {}, {}, {}
    for b in range(bb):
        for m in range(n_slabs):
            lanes = slice(m * LANES, (m + 1) * LANES)
            prev = [xp_scr[b, m, j * SUBLANES:(j + 1) * SUBLANES, :] for j in range(L)]
            for c in range(nc):
                cur = chunk_groups(
                    lambda j: qkv_ref[b, m, pl.ds(c * C + j, SUBLANES, stride=L), :])
                cache = {}
                y = shifted(cur, prev, 3, cache) * wconv_ref[0:1, lanes]
                y = y + shifted(cur, prev, 2, cache) * wconv_ref[1:2, lanes]
                y = y + shifted(cur, prev, 1, cache) * wconv_ref[2:3, lanes]
                y = y + jnp.concatenate(cur, axis=0) * wconv_ref[3:4, lanes]
                ys[b, c, m] = _silu(y)
                prev = cur
            xp_scr[b, m] = jnp.concatenate(prev, axis=0)

        for c in range(nc):
            ab = jnp.concatenate(chunk_groups(
                lambda j: ab_ref[b, pl.ds(c * C + j, SUBLANES, stride=L), :]), axis=0)
            t = ab + dt_b
            g = neg_a * (jnp.maximum(t, 0.0) + jnp.log1p(jnp.exp(-jnp.abs(t))))
            betas[b, c] = jax.nn.sigmoid(ab)
            gc = jnp.concatenate(_cumsum_time(
                [g[j * SUBLANES:(j + 1) * SUBLANES] for j in range(L)]), axis=0)
            gcs[b, c] = gc
            gcts[b, c] = jnp.transpose(jnp.concatenate(
                [gc, jnp.zeros((LANES - C, LANES), F32)], axis=0))

        scv = sc_ref[b]
        prod = scv[:, SC_WIDTH:2 * SC_WIDTH] * scv[:, 2 * SC_WIDTH:3 * SC_WIDTH]
        pext = jnp.concatenate([scp_scr[b], prod], axis=0)
        cu = pltpu.roll(pext, 2, axis=0)[pad:] * wsc_ref[0:1, :]
        cu = cu + pltpu.roll(pext, 1, axis=0)[pad:] * wsc_ref[1:2, :]
        cu = cu + prod * wsc_ref[2:3, :]
        scp_scr[b] = prod[R - pad:R, :]
        oc_ref[b] = scv[:, 0:SC_WIDTH] * cu

    problems = [(c, b, h) for c in range(nc) for b in range(bb) for h in range(GDN_HEADS)]
    every = range(len(problems))
    qs, ks, vs, kbs, bts, egs, e_incls, e_stricts, kds, gls = ([] for _ in range(10))
    for c, b, h in problems:
        gc = gcs[b, c]
        q = ys[b, c, h]
        k = ys[b, c, GDN_HEADS + h]
        q = q * lax.rsqrt(jnp.sum(q * q, axis=-1, keepdims=True) + EPS) * (GDN_DK ** -0.5)
        k = k * lax.rsqrt(jnp.sum(k * k, axis=-1, keepdims=True) + EPS)
        gci = gc[:, h:h + 1]
        gcj = gcts[b, c][h:h + 1, 0:C]
        g_last = gc[C - 1:C, h:h + 1]
        e_incl = jnp.exp(jnp.where(incl, gci - gcj, NEG_INF))
        qs.append(q)
        ks.append(k)
        kbs.append(k.astype(BF16))
        vs.append(ys[b, c, 2 * GDN_HEADS + h])
        bts.append(betas[b, c][:, GDN_HEADS + h:GDN_HEADS + h + 1])
        egs.append(jnp.exp(gci))
        e_incls.append(e_incl)
        e_stricts.append(jnp.where(strict, e_incl, 0.0))
        kds.append(k * jnp.exp(g_last - gci))
        gls.append(jnp.exp(g_last))

    kks = [_dot_nt(kbs[i], kbs[i]) for i in every]
    qks = [_dot_nt(qs[i], kbs[i]) * e_incls[i] for i in every]
    p_pows = [-(bts[i] * kks[i] * e_stricts[i]) for i in every]
    n_accs = list(p_pows)
    for _ in range(n_sq):
        pbs = [p.astype(BF16) for p in p_pows]
        p_pows = [_dot(pb, pb) for pb in pbs]
        n_accs = [n_accs[i] + p_pows[i] + _dot(p_pows[i], n_accs[i]) for i in every]
    rhss = [jnp.concatenate([bts[i] * vs[i], (bts[i] * egs[i]) * ks[i]], axis=-1) for i in every]
    sols = [rhss[i] + _dot(n_accs[i], rhss[i]) for i in every]

    pairs = [(b, h) for b in range(bb) for h in range(GDN_HEADS)]
    here = range(len(pairs))
    states = [s_scr[b, h] for b, h in pairs]
    for c in range(nc):
        base = c * len(pairs)
        sbs = [s.astype(BF16) for s in states]
        o_inter = [_dot(qs[base + j] * egs[base + j], sbs[j]) for j in here]
        us = [sols[base + j][:, :GDN_DV] - _dot(sols[base + j][:, GDN_DV:], sbs[j]) for j in here]
        outs = [o_inter[j] + _dot(qks[base + j], us[j]) for j in here]
        states = [gls[base + j] * states[j] + _dot_tn(kds[base + j], us[j]) for j in here]
        for j, (b, h) in enumerate(pairs):
            zh = jnp.concatenate(chunk_groups(
                lambda i: z_ref[b, h, pl.ds(c * C + i, SUBLANES, stride=L), :]), axis=0)
            res = _rms(outs[j], gn_ref[...]) * _silu(zh)
            for i in range(L):
                oa_ref[b, h, pl.ds(c * C + i, SUBLANES, stride=L), :] = res[
                    i * SUBLANES:(i + 1) * SUBLANES]
    for j, (b, h) in enumerate(pairs):
        s_scr[b, h] = states[j]

    @pl.when(n == pl.num_programs(1) - 1)
    def _fin():
        sout_ref[...] = s_scr[...]
        scout_ref[...] = scp_scr[:, pad - (SC_CONV - 1):pad, :]


def _gdn(qkv, ab, z, sc, l, p, states):
    B, n_slabs, T, _ = qkv.shape
    C = CHUNK if T % CHUNK == 0 else T
    assert C % SUBLANES == 0
    bb = 2 if B % 2 == 0 else 1
    nc = GDN_CHUNKS_PER_STEP if (T // C) % GDN_CHUNKS_PER_STEP == 0 else 1
    tok = lambda w: pl.BlockSpec((bb, nc * C, w), lambda i, n: (i, n, 0))
    slabs = lambda k: pl.BlockSpec((bb, k, nc * C, LANES), lambda i, n: (i, 0, n, 0))
    per_b = lambda *s: pl.BlockSpec((bb,) + s, lambda i, n: (i,) + (0,) * len(s))
    per_lb = lambda *s: pl.BlockSpec((None, bb) + s, lambda i, n: (l, i) + (0,) * len(s))
    in_specs = [slabs(n_slabs), tok(AB_W), slabs(GDN_HEADS), tok(SC_W),
                _lspec(l, (GDN_CONV, GDN_CONV_CH)), _lspec(l, (SC_CONV, SC_WIDTH)),
                _lspec(l, (SUBLANES, LANES)), _lspec(l, (1, GDN_DV))]
    args = [qkv, ab, z, sc, p['w_gdn_conv'], p['w_sc_conv'], p['hp'], p['gdn_norm_g']]
    if states is not None:
        in_specs += [per_lb(GDN_HEADS, GDN_DK, GDN_DV), per_lb(GDN_CONV - 1, GDN_CONV_CH),
                     per_lb(SC_CONV - 1, SC_WIDTH)]
        args += list(states)
    return pl.pallas_call(
        functools.partial(_gdn_kernel, bb=bb, C=C, nc=nc, has_state=states is not None),
        grid=(B // bb, T // (nc * C)),
        in_specs=in_specs,
        out_specs=[slabs(GDN_HEADS), tok(SC_WIDTH),
                   per_b(GDN_HEADS, GDN_DK, GDN_DV), per_b(SC_CONV - 1, SC_WIDTH)],
        out_shape=[jax.ShapeDtypeStruct((B, GDN_HEADS, T, GDN_DV), F32),
                   jax.ShapeDtypeStruct((B, T, SC_WIDTH), F32),
                   jax.ShapeDtypeStruct((B, GDN_HEADS, GDN_DK, GDN_DV), F32),
                   jax.ShapeDtypeStruct((B, SC_CONV - 1, SC_WIDTH), F32)],
        scratch_shapes=[pltpu.VMEM((bb, GDN_HEADS, GDN_DK, GDN_DV), F32),
                        pltpu.VMEM((bb, n_slabs, C, LANES), F32),
                        pltpu.VMEM((bb, SUBLANES, SC_WIDTH), F32)],
        compiler_params=_params(("arbitrary", "arbitrary")),
        name="gdn",
    )(*args)


def _swa_norm_k(x, gk):
    return jnp.concatenate(
        [_rms(x[:, SWA_Q + i * SWA_HD:SWA_Q + (i + 1) * SWA_HD], gk) for i in range(SWA_KV_HEADS)],
        axis=-1)


def _swa_norm_q(x, gq):
    return [_rms(x[:, h * SWA_HD:(h + 1) * SWA_HD], gq) for h in range(SWA_HEADS)]


def _group_queries(qn, g, r0, r1, sinks):
    heads = range(g * SWA_GROUP, (g + 1) * SWA_GROUP)
    q = jnp.concatenate([qn[h][r0:r1] for h in heads], axis=0)
    sink = jnp.concatenate(
        [jnp.broadcast_to(sinks[:, h:h + 1], (r1 - r0, 1)) for h in heads], axis=0)
    return q, sink


def _attend_many(problems):
    scores = [_dot_nt(q, k) * (SWA_HD ** -0.5) for q, k, _, _, _ in problems]
    probs = []
    for s, (_, _, _, valid, sink) in zip(scores, problems):
        if valid is not None:
            s = jnp.where(valid, s, NEG_INF)
        m = jnp.maximum(jnp.max(s, axis=-1, keepdims=True), sink)
        p = jnp.exp(s - m)
        probs.append(p / (jnp.sum(p, axis=-1, keepdims=True) + jnp.exp(sink - m)))
    return [_dot(p, pr[2]) for p, pr in zip(probs, problems)]


def _ungroup(outs, rows):
    return jnp.concatenate(
        [o[j * rows:(j + 1) * rows] for o in outs for j in range(SWA_GROUP)], axis=-1)


def _split_bf16(x):
    hi = x.astype(BF16)
    return hi, (x - hi.astype(F32)).astype(BF16)


def _sum_dot(x, ones):
    hi, lo = _split_bf16(x)
    return (jnp.dot(hi, ones, preferred_element_type=F32)
            + jnp.dot(lo, ones, preferred_element_type=F32))


def _head_rms(x, g_tiled):
    w = x.shape[1]
    same_head = (lax.broadcasted_iota(jnp.int32, (w, w), 0) // SWA_HD
                 == lax.broadcasted_iota(jnp.int32, (w, w), 1) // SWA_HD)
    ss = _sum_dot(x * x, jnp.where(same_head, 1.0, 0.0).astype(BF16))
    return x * lax.rsqrt(ss * (1.0 / SWA_HD) + EPS) * g_tiled


def _swa_prompt_kernel(x_ref, halo_ref, gq_ref, gk_ref, sink_ref, o_ref, kn_ref, *, qb):
    i = pl.program_id(1)
    band = (WIN_CHUNKS + 1) * CHUNK
    n_chunks = qb // CHUNK
    x = x_ref[...]
    hx = halo_ref[...]
    kcols = slice(SWA_Q, SWA_Q + SWA_KV)
    vcols = slice(SWA_Q + SWA_KV, SWA_W)
    qn = _head_rms(x[:, 0:SWA_Q], gq_ref[...])
    k_all = _head_rms(jnp.concatenate([hx[:, kcols], x[:, kcols]], axis=0), gk_ref[...])
    v_all = jnp.concatenate([hx[:, vcols], x[:, vcols]], axis=0)
    kn_ref[...] = k_all[WINDOW:]

    lane_half = lax.broadcasted_iota(jnp.int32, (1, SWA_KV), 1) // SWA_HD

    def variants(a):
        rot = pltpu.roll(a, SWA_HD, axis=1)
        return [[jnp.where(lane_half == j, a if j == g else rot, 0.0).astype(BF16)
                 for j in range(SWA_GROUP)] for g in range(SWA_KV_HEADS)]

    k_var, v_var = variants(k_all), variants(v_all)
    v_ones = [[jnp.concatenate([v, jnp.ones((WINDOW + qb, SWA_KV), BF16)], axis=1) for v in vg]
              for vg in v_var]
    local_chunk = lax.broadcasted_iota(jnp.int32, (1, band), 1) // CHUNK
    problems = [(c, g, j) for c in range(n_chunks) for g in range(SWA_KV_HEADS)
                for j in range(SWA_GROUP)]
    scores = []
    for c, g, j in problems:
        q = qn[c * CHUNK:(c + 1) * CHUNK, g * SWA_KV:(g + 1) * SWA_KV]
        scores.append(_dot_nt(q, k_var[g][j][c * CHUNK:c * CHUNK + band]) * (SWA_HD ** -0.5))
    ps, tails = [], []
    for s, (c, g, j) in zip(scores, problems):
        valid = local_chunk + (i * n_chunks + c - WIN_CHUNKS) >= 0
        s = jnp.where(valid, s, NEG_INF)
        sink = sink_ref[:, g * SWA_GROUP + j:g * SWA_GROUP + j + 1]
        m = jnp.maximum(jnp.max(s, axis=-1, keepdims=True), sink)
        ps.append(jnp.exp(s - m))
        tails.append(jnp.exp(sink - m))
    outs = []
    for p, tail, (c, g, j) in zip(ps, tails, problems):
        p_hi, p_lo = _split_bf16(p)
        pv_sum = jnp.dot(jnp.concatenate([p_hi, p_lo], axis=0),
                         v_ones[g][j][c * CHUNK:c * CHUNK + band], preferred_element_type=F32)
        denom = (pv_sum[:CHUNK, SWA_KV:] + pv_sum[CHUNK:, SWA_KV:]) + tail
        outs.append(pv_sum[:CHUNK, :SWA_KV] / denom)
    for c in range(n_chunks):
        for g in range(SWA_KV_HEADS):
            k0 = (c * SWA_KV_HEADS + g) * SWA_GROUP
            o_ref[c * CHUNK:(c + 1) * CHUNK, g * SWA_KV:(g + 1) * SWA_KV] = sum(
                outs[k0 + 1:k0 + SWA_GROUP], outs[k0])


def _swa_prompt(swa, l, p):
    B, T, _ = swa.shape
    qb = min(SWA_QUERY_BLOCK, T)
    assert T % qb == 0 and qb % WINDOW == 0
    halo_per_block = qb // WINDOW
    return pl.pallas_call(
        functools.partial(_swa_prompt_kernel, qb=qb),
        grid=(B, T // qb),
        in_specs=[pl.BlockSpec((None, qb, SWA_W), lambda b, i: (b, i, 0)),
                  pl.BlockSpec((None, WINDOW, SWA_W),
                               lambda b, i: (b, jnp.maximum(i * halo_per_block - 1, 0), 0)),
                  _lspec(l, (1, SWA_Q)), _lspec(l, (1, SWA_KV)), _lspec(l, (1, LANES))],
        out_specs=[pl.BlockSpec((None, qb, SWA_Q), lambda b, i: (b, i, 0)),
                   pl.BlockSpec((None, qb, SWA_KV), lambda b, i: (b, i, 0))],
        out_shape=[jax.ShapeDtypeStruct((B, T, SWA_Q), F32),
                   jax.ShapeDtypeStruct((B, T, SWA_KV), F32)],
        compiler_params=_params(("arbitrary", "arbitrary")),
        name="swa_prompt",
    )(swa, swa, p['swa_q_norm_g_tiled'], p['swa_k_norm_g_tiled'], p['swa_sinks'])


def _swa_sample_kernel(x_ref, kp_ref, vp_ref, gq_ref, gk_ref, sink_ref, o_ref, kn_ref, *, bb, T):
    problems = []
    for b in range(bb):
        x = x_ref[b]
        kn = _swa_norm_k(x, gk_ref[...])
        kn_ref[b] = kn
        k_all = jnp.concatenate([kp_ref[b], kn], axis=0)
        v_all = jnp.concatenate([vp_ref[b], x[:, SWA_Q + SWA_KV:SWA_W]], axis=0)
        qn = _swa_norm_q(x, gq_ref[...])
        for g in range(SWA_KV_HEADS):
            q, sink = _group_queries(qn, g, 0, T, sink_ref[...])
            cols = slice(g * SWA_HD, (g + 1) * SWA_HD)
            problems.append((q, k_all[:, cols], v_all[:, cols], None, sink))
    outs = _attend_many(problems)
    for b in range(bb):
        o_ref[b] = _ungroup(outs[b * SWA_KV_HEADS:(b + 1) * SWA_KV_HEADS], T)


def _swa_sample(swa, k_past, v_past, l, p):
    B, T, _ = swa.shape
    past = k_past.shape[2]
    bb = 4 if B % 4 == 0 else 1
    blk = lambda r, w: pl.BlockSpec((bb, r, w), lambda b: (b, 0, 0))
    cache = pl.BlockSpec((None, bb, past, SWA_KV), lambda b: (l, b, 0, 0))
    return pl.pallas_call(
        functools.partial(_swa_sample_kernel, bb=bb, T=T),
        grid=(B // bb,),
        in_specs=[blk(T, SWA_W), cache, cache,
                  _lspec(l, (1, SWA_HD)), _lspec(l, (1, SWA_HD)), _lspec(l, (1, LANES))],
        out_specs=[blk(T, SWA_Q), blk(T, SWA_KV)],
        out_shape=[jax.ShapeDtypeStruct((B, T, SWA_Q), F32),
                   jax.ShapeDtypeStruct((B, T, SWA_KV), F32)],
        compiler_params=_params(("arbitrary",)),
        name="swa_sample",
    )(swa, k_past, v_past, p['swa_q_norm_g'], p['swa_k_norm_g'], p['swa_sinks'])


def _mem_kv_kernel(m_ref, g_ref, wk_ref, wv_ref, gk_ref, k_ref, v_ref):
    m = _rms(m_ref[...], g_ref[...]).astype(BF16)
    k = jnp.dot(m, wk_ref[...], preferred_element_type=F32)
    k_ref[...] = jnp.concatenate(
        [_rms(k[:, h * MEM_HD:(h + 1) * MEM_HD], gk_ref[...]) for h in range(MEM_HEADS)], axis=-1)
    v_ref[...] = jnp.dot(m, wv_ref[...], preferred_element_type=F32)


def _mem_kv(mem, l, p):
    B, M, _ = mem.shape
    blk = pl.BlockSpec((None, M, D_MODEL), lambda b: (b, 0, 0))
    return pl.pallas_call(
        _mem_kv_kernel,
        grid=(B,),
        in_specs=[blk, _lspec(l, (1, D_MODEL)), _lspec(l, (D_MODEL, D_MODEL)),
                  _lspec(l, (D_MODEL, D_MODEL)), _lspec(l, (1, MEM_HD))],
        out_specs=[blk, blk],
        out_shape=[jax.ShapeDtypeStruct((B, M, D_MODEL), F32)] * 2,
        compiler_params=_params(("arbitrary",)),
        name="mem_kv",
    )(mem, p['mem_in_norm_g'], p['w_mk'], p['w_mv'], p['mem_k_norm_g'])


def _post_kernel(x_ref, a_ref, b_ref, c_ref, wo_ref, gm_ref, wq_ref, gq_ref, mk_ref, mv_ref,
                 wmo_ref, o_ref, *, nb, tm, n_sub):
    ts = tm // n_sub
    probs = [(b, slice(hd * MEM_HD, (hd + 1) * MEM_HD)) for b in range(nb) for hd in range(MEM_HEADS)]
    hs, scores = [], []
    for sub in range(n_sub):
        rows = lambda ref: _block_rows(ref, nb, sub * ts, ts)
        h = rows(x_ref)
        h = h + (jnp.dot(rows(a_ref).astype(BF16), wo_ref[0:GDN_V, :], preferred_element_type=F32)
                 + jnp.dot(rows(b_ref).astype(BF16), wo_ref[GDN_V:GDN_V + SWA_Q, :],
                           preferred_element_type=F32)
                 + jnp.dot(rows(c_ref).astype(BF16), wo_ref[GDN_V + SWA_Q:, :],
                           preferred_element_type=F32))
        hs.append(h)
        hn = _rms(h, gm_ref[...]).astype(BF16)
        q = jnp.dot(hn, wq_ref[...], preferred_element_type=F32)
        qhs = [_rms(q[b * ts:(b + 1) * ts, sl], gq_ref[...]) for b, sl in probs]
        scores.append([_dot_nt(qh, mk_ref[b, :, sl]) * (MEM_HD ** -0.5)
                       for qh, (b, sl) in zip(qhs, probs)])
    ps = []
    for sub in range(n_sub):
        ps.append([])
        for s in scores[sub]:
            p = jnp.exp(s - jnp.max(s, axis=-1, keepdims=True))
            ps[sub].append(p / jnp.sum(p, axis=-1, keepdims=True))
    for sub in range(n_sub):
        outs = [_dot(p, mv_ref[b, :, sl]) for p, (b, sl) in zip(ps[sub], probs)]
        o = jnp.concatenate(
            [jnp.concatenate(outs[b * MEM_HEADS:(b + 1) * MEM_HEADS], axis=-1) for b in range(nb)],
            axis=0).astype(BF16)
        res = hs[sub] + jnp.dot(o, wmo_ref[...], preferred_element_type=F32)
        for b in range(nb):
            o_ref[b, sub * ts:(sub + 1) * ts, :] = res[b * ts:(b + 1) * ts]


def _post(xs, oa, ob, oc, mk, mv, lm, l, p):
    B, T, _ = xs.shape
    nb, tm, n_sub = _sub_tiles(B, T)
    tok = lambda w: _tok_spec(nb, tm, w)
    mem = pl.BlockSpec((None, nb, MEM_TOKENS, D_MODEL), lambda b, i: (lm, b, 0, 0))
    sq = _lspec(l, (D_MODEL, D_MODEL), single_buffer=True)
    return pl.pallas_call(
        functools.partial(_post_kernel, nb=nb, tm=tm, n_sub=n_sub),
        grid=(B // nb, T // tm),
        in_specs=[tok(D_MODEL),
                  pl.BlockSpec((nb, GDN_HEADS, tm, GDN_DV), lambda b, i: (b, 0, i, 0)),
                  tok(SWA_Q), tok(SC_WIDTH), sq,
                  _lspec(l, (1, D_MODEL)), sq, _lspec(l, (1, MEM_HD)), mem, mem, sq],
        out_specs=tok(D_MODEL),
        out_shape=jax.ShapeDtypeStruct((B, T, D_MODEL), F32),
        compiler_params=_params(("arbitrary", "arbitrary")),
        name="post",
    )(xs, oa, ob, oc, p['w_o'], p['norm_mem_g'], p['w_mq'], p['mem_q_norm_g'], mk, mv, p['w_mo'])


def _ffn_kernel(*refs, nb, tm, n_sub, has_state):
    h_ref, g_ref, wup_ref, wc_ref, wdn_ref = refs[:5]
    buf_ref = refs[5] if has_state else None
    o_ref, st_ref, carry_scr = refs[5 + int(has_state):]
    i = pl.program_id(1)
    ts = tm // n_sub

    @pl.when(i == 0)
    def _init():
        carry_scr[...] = jnp.zeros_like(carry_scr)
        if has_state:
            for b in range(nb):
                u_m2, u_m1 = buf_ref[b, 0:1, :], buf_ref[b, 1:2, :]
                carry_scr[b, 0:1, :] = u_m1 * wc_ref[0:1, :]
                carry_scr[b, 1:2, :] = u_m1 * wc_ref[1:2, :] + u_m2 * wc_ref[0:1, :]
                carry_scr[b, 2:4, :] = buf_ref[b]

    def delayed(x, carry_row, cols):
        rolled = pltpu.roll(x, 1, axis=0)
        first_row = lax.broadcasted_iota(jnp.int32, (SUBLANES, x.shape[1]), 0) == 0
        pieces = []
        for b in range(nb):
            r0 = b * ts
            pieces.append(jnp.where(first_row, carry_scr[b, carry_row:carry_row + 1, cols],
                                    rolled[r0:r0 + SUBLANES]))
            pieces.append(rolled[r0 + SUBLANES:r0 + ts])
        return jnp.concatenate(pieces, axis=0)

    def conv_act(xn, lo, hi):
        conv = []
        for base in (0, D_FF):
            cols = slice(base + lo, base + hi)
            u = jnp.dot(xn, wup_ref[:, cols], preferred_element_type=F32)
            first = u * wc_ref[0:1, cols]
            mid = u * wc_ref[1:2, cols] + delayed(first, 0, cols)
            y = delayed(mid, 1, cols) + u * wc_ref[2:3, cols]
            for b in range(nb):
                last = (b + 1) * ts - 1
                carry_scr[b, 0:1, cols] = first[last:last + 1]
                carry_scr[b, 1:2, cols] = mid[last:last + 1]
                carry_scr[b, 2:4, cols] = u[last - 1:last + 1]
            conv.append(y)
        return (_silu(conv[0]) * conv[1]).astype(BF16)

    def down(acc, act, lo, hi):
        return acc + jnp.dot(act, wdn_ref[lo:hi, :], preferred_element_type=F32)

    n_ch = len(FFN_COL_CHUNKS)
    for sub in range(n_sub):
        t0 = sub * ts
        hs = _block_rows(h_ref, nb, t0, ts)
        xn = _rms(hs, g_ref[...]).astype(BF16)
        acc = hs
        act = conv_act(xn, *FFN_COL_CHUNKS[0])
        for k in range(n_ch):
            nxt = conv_act(xn, *FFN_COL_CHUNKS[k + 1]) if k + 1 < n_ch else None
            acc = down(acc, act, *FFN_COL_CHUNKS[k])
            act = nxt
        for b in range(nb):
            o_ref[b, t0:t0 + ts, :] = acc[b * ts:(b + 1) * ts]

    @pl.when(i == pl.num_programs(1) - 1)
    def _fin():
        for b in range(nb):
            st_ref[b] = carry_scr[b, 2:4, :]


def _ffn(h, l, p, state):
    B, T, _ = h.shape
    nb, tm, n_sub = _sub_tiles(B, T)
    assert (tm // n_sub) % (2 * SUBLANES) == 0
    tok = _tok_spec(nb, tm, D_MODEL)
    in_specs = [tok, _lspec(l, (1, D_MODEL)), _lspec(l, (D_MODEL, 2 * D_FF), single_buffer=True),
                _lspec(l, (FFN_CONV, 2 * D_FF)), _lspec(l, (D_FF, D_MODEL), single_buffer=True)]
    args = [h, p['norm_ffn_g'], p['w_up'], p['w_ffn_conv'], p['w_down']]
    if state is not None:
        in_specs.append(pl.BlockSpec((None, nb, FFN_CONV - 1, 2 * D_FF), lambda b, i: (l, b, 0, 0)))
        args.append(state)
    return pl.pallas_call(
        functools.partial(_ffn_kernel, nb=nb, tm=tm, n_sub=n_sub, has_state=state is not None),
        grid=(B // nb, T // tm),
        in_specs=in_specs,
        out_specs=[tok, pl.BlockSpec((nb, FFN_CONV - 1, 2 * D_FF), lambda b, i: (b, 0, 0))],
        out_shape=[jax.ShapeDtypeStruct(h.shape, F32),
                   jax.ShapeDtypeStruct((B, FFN_CONV - 1, 2 * D_FF), F32)],
        scratch_shapes=[pltpu.VMEM((nb, SUBLANES, 2 * D_FF), F32)],
        compiler_params=_params(("arbitrary", "arbitrary")),
        name="ffn",
    )(*args)


def _layer(x, l, p, mk, mv, lm, gdn_states, ffn_state, swa_cache, keep):
    B, T, _ = x.shape
    qkv, z, swa, sc, ab = _in_proj(x, l, p)
    oa, oc, s_new, sc_new = _gdn(qkv, ab, z, sc, l, p, gdn_states)
    if swa_cache is None:
        ob, kn = _swa_prompt(swa, l, p)
    else:
        ob, kn = _swa_sample(swa, swa_cache[0], swa_cache[1], l, p)
    h = _post(x, oa, ob, oc, mk, mv, lm, l, p)
    h, f_new = _ffn(h, l, p, ffn_state)
    gb_new = qkv[:, :, T - (GDN_CONV - 1):, :].transpose(0, 2, 1, 3).reshape(
        B, GDN_CONV - 1, GDN_CONV_CH)
    sk = kn[:, T - keep:].reshape(B, keep, SWA_KV_HEADS, SWA_HD)
    sv = swa[:, T - keep:, SWA_Q + SWA_KV:].reshape(B, keep, SWA_KV_HEADS, SWA_HD)
    return h, s_new, gb_new, sc_new, f_new, sk, sv


def kernel(x_prompt, x_sample, mem_prompt, state_gdn, state_gdn_conv, cache_swa_k, cache_swa_v, state_sc_conv, cache_mem_k, cache_mem_v, state_ffn_conv, norm_mix_g, w_in, w_gdn_conv, gdn_a_log, gdn_dt_bias, gdn_norm_g, swa_q_norm_g, swa_k_norm_g, swa_sinks, w_sc_conv, w_o, norm_mem_g, mem_in_norm_g, w_mq, w_mk, w_mv, mem_q_norm_g, mem_k_norm_g, w_mo, norm_ffn_g, w_up, w_ffn_conv, w_down):
    depth = w_in.shape[0]
    Bp, Tp, _ = x_prompt.shape
    Bs, Ts, _ = x_sample.shape
    keep = min(WINDOW, Tp)
    past = cache_swa_k.shape[2]

    hp = jnp.zeros((depth, SUBLANES, LANES), F32)
    hp = hp.at[:, 0, :GDN_HEADS].set(gdn_a_log).at[:, 1, :GDN_HEADS].set(gdn_dt_bias)
    sinks = jnp.zeros((depth, 1, LANES), F32).at[:, 0, :SWA_HEADS].set(swa_sinks)
    row = lambda a: a[:, None, :]
    p = {
        'norm_mix_g': row(norm_mix_g), 'w_in': w_in.astype(BF16), 'w_gdn_conv': w_gdn_conv, 'hp': hp,
        'gdn_norm_g': row(gdn_norm_g), 'swa_q_norm_g': row(swa_q_norm_g),
        'swa_k_norm_g': row(swa_k_norm_g), 'swa_sinks': sinks,
        'swa_q_norm_g_tiled': row(jnp.tile(swa_q_norm_g, (1, SWA_HEADS))),
        'swa_k_norm_g_tiled': row(jnp.tile(swa_k_norm_g, (1, SWA_KV_HEADS))), 'w_sc_conv': w_sc_conv,
        'w_o': w_o.astype(BF16), 'norm_mem_g': row(norm_mem_g), 'mem_in_norm_g': row(mem_in_norm_g),
        'w_mq': w_mq.astype(BF16), 'w_mk': w_mk.astype(BF16), 'w_mv': w_mv.astype(BF16),
        'mem_q_norm_g': row(mem_q_norm_g), 'mem_k_norm_g': row(mem_k_norm_g),
        'w_mo': w_mo.astype(BF16), 'norm_ffn_g': row(norm_ffn_g), 'w_up': w_up.astype(BF16),
        'w_ffn_conv': w_ffn_conv, 'w_down': w_down.astype(BF16),
    }
    cache_k = cache_swa_k.reshape(depth, Bs, past, SWA_KV)
    cache_v = cache_swa_v.reshape(depth, Bs, past, SWA_KV)
    mem_k_s = cache_mem_k.reshape(depth, Bs, MEM_TOKENS, D_MODEL)
    mem_v_s = cache_mem_v.reshape(depth, Bs, MEM_TOKENS, D_MODEL)

    hp_, hs_ = x_prompt, x_sample
    outs_p = {k: [] for k in ('S', 'gb', 'sk', 'sv', 'sb', 'mk', 'mv', 'fb')}
    outs_s = {k: [] for k in ('S', 'gb', 'sk', 'sv', 'sb', 'fb')}
    for l in range(depth):
        mk, mv = _mem_kv(mem_prompt, l, p)
        hp_, S, gb, sb, fb, sk, sv = _layer(hp_, l, p, mk[None], mv[None], 0, None, None, None, keep)
        outs_p['S'].append(S); outs_p['gb'].append(gb); outs_p['sb'].append(sb); outs_p['fb'].append(fb)
        outs_p['sk'].append(sk); outs_p['sv'].append(sv)
        outs_p['mk'].append(mk.reshape(Bp, MEM_TOKENS, MEM_HEADS, MEM_HD))
        outs_p['mv'].append(mv.reshape(Bp, MEM_TOKENS, MEM_HEADS, MEM_HD))
        hs_, S, gb, sb, fb, sk, sv = _layer(
            hs_, l, p, mem_k_s, mem_v_s, l, (state_gdn, state_gdn_conv, state_sc_conv),
            state_ffn_conv, (cache_k, cache_v), Ts)
        outs_s['S'].append(S); outs_s['gb'].append(gb); outs_s['sb'].append(sb); outs_s['fb'].append(fb)
        outs_s['sk'].append(sk); outs_s['sv'].append(sv)
    st = jnp.stack
    return (hp_, hs_,
            st(outs_p['S']), st(outs_p['gb']), st(outs_p['sk']), st(outs_p['sv']), st(outs_p['sb']),
            st(outs_p['mk']), st(outs_p['mv']), st(outs_p['fb']),
            st(outs_s['S']), st(outs_s['gb']), st(outs_s['sk']), st(outs_s['sv']), st(outs_s['sb']),
            st(outs_s['fb']))
```
